```python
import jax
import jax.numpy as jnp
from jax import lax
import numpy as np

D_MODEL = 4096
BATCH = 4
SEQ = 2048
DEPTH = 4
DEC_BATCH = 128
DEC_SEQ = 8
PAST_LEN = 8192
PAGE_SIZE = 128

GDN_HEADS = 16
GDN_DK = 128
GDN_DV = 128
GDN_CONV = 4
GDN_CHUNK = 64
GDN_QK_W = GDN_HEADS * GDN_DK
GDN_V_W = GDN_HEADS * GDN_DV
CONV_DIM = 2 * GDN_QK_W + GDN_V_W

MLA_HEADS = 16
MLA_Q_LORA = 768
MLA_KV_LORA = 256
MLA_NOPE = 128
MLA_ROPE = 64
MLA_V = 128
MLA_SCALE = (MLA_NOPE + MLA_ROPE) ** -0.5
ROPE_THETA = 10000.0
Q_BLOCK = 128

D_FF = 4 * D_MODEL
N_MOD = 6
EPS = 1e-6

IN_SPLITS = (CONV_DIM, GDN_V_W, GDN_HEADS, GDN_HEADS, MLA_Q_LORA, MLA_KV_LORA, MLA_ROPE, D_MODEL, D_MODEL)
IN_OFFSETS = tuple(int(v) for v in np.cumsum(IN_SPLITS)[:-1])
W_IN_COLS = int(sum(IN_SPLITS))

kernel_name = 'hybrid_gdn_mla_decoder_step'


def rmsnorm(x, g):
    xf = x.astype(jnp.float32)
    y = xf * lax.rsqrt(jnp.mean(xf * xf, axis=-1, keepdims=True) + EPS)
    return (y * g.astype(jnp.float32)).astype(x.dtype)


def l2norm(x):
    xf = x.astype(jnp.float32)
    return (xf * lax.rsqrt(jnp.sum(xf * xf, axis=-1, keepdims=True) + EPS)).astype(x.dtype)


def rope_tables(pos):
    half = MLA_ROPE // 2
    inv_freq = ROPE_THETA ** (-jnp.arange(half, dtype=jnp.float32) / half)
    ang = pos.astype(jnp.float32)[:, None] * inv_freq[None, :]
    return jnp.cos(ang), jnp.sin(ang)


def apply_rope(x, cos, sin):
    half = MLA_ROPE // 2
    x1 = x[..., :half].astype(jnp.float32)
    x2 = x[..., half:].astype(jnp.float32)
    return jnp.concatenate([x1 * cos - x2 * sin, x2 * cos + x1 * sin], axis=-1).astype(x.dtype)


def causal_short_conv(u, buf, w):
    T = u.shape[1]
    full = jnp.concatenate([buf.astype(u.dtype), u], axis=1)
    out = full[:, 0:T, :] * w[0]
    for i in range(1, GDN_CONV):
        out = out + full[:, i:i + T, :] * w[i]
    return jax.nn.silu(out), full[:, T:, :]


def gated_delta_chunked(q, k, v, g, beta, S0):
    B, T, H, DK = q.shape
    DV = v.shape[-1]
    C = GDN_CHUNK if T % GDN_CHUNK == 0 else T
    NC = T // C
    f32 = jnp.float32

    def chunk4(a):
        return a.astype(f32).reshape(B, NC, C, H, a.shape[-1]).transpose(0, 1, 3, 2, 4)

    def chunk3(a):
        return a.astype(f32).reshape(B, NC, C, H).transpose(0, 1, 3, 2)

    qc = chunk4(q) * (DK ** -0.5)
    kc = chunk4(k)
    vc = chunk4(v)
    bc = chunk3(beta)
    Gc = jnp.cumsum(chunk3(g), axis=-1)
    tri = jnp.tril(jnp.ones((C, C), dtype=bool))
    strict = jnp.tril(jnp.ones((C, C), dtype=bool), -1)
    diff = Gc[..., :, None] - Gc[..., None, :]
    decay = jnp.where(tri, jnp.exp(jnp.where(tri, diff, 0.0)), 0.0)
    kk = jnp.einsum('bnhid,bnhjd->bnhij', kc, kc)
    L = jnp.where(strict, bc[..., :, None] * kk * decay, 0.0)
    eye = jnp.broadcast_to(jnp.eye(C, dtype=f32), L.shape)
    Tm = lax.linalg.triangular_solve(eye + L, eye, left_side=True, lower=True, unit_diagonal=True)
    u = jnp.einsum('bnhij,bnhjd->bnhid', Tm, vc * bc[..., None])
    w = jnp.einsum('bnhij,bnhjd->bnhid', Tm, kc * (bc * jnp.exp(Gc))[..., None])
    qk = jnp.einsum('bnhid,bnhjd->bnhij', qc, kc) * decay
    q_dec = qc * jnp.exp(Gc)[..., None]
    k_dec = kc * jnp.exp(Gc[..., -1:] - Gc)[..., None]
    g_last = jnp.exp(Gc[..., -1])

    def step(S, inp):
        u_c, w_c, qk_c, qd_c, kd_c, gl_c = inp
        v_new = u_c - jnp.einsum('bhck,bhkv->bhcv', w_c, S)
        o = jnp.einsum('bhck,bhkv->bhcv', qd_c, S) + jnp.einsum('bhij,bhjv->bhiv', qk_c, v_new)
        S = S * gl_c[..., None, None] + jnp.einsum('bhck,bhcv->bhkv', kd_c, v_new)
        return S, o

    xs = tuple(jnp.moveaxis(a, 1, 0) for a in (u, w, qk, q_dec, k_dec, g_last))
    S_fin, o = lax.scan(step, S0.astype(f32), xs)
    o = o.transpose(1, 0, 3, 2, 4).reshape(B, T, H, DV)
    return o, S_fin.astype(S0.dtype)


def gated_deltanet(qkv_raw, z, a, b, conv_buf, S0, conv_w, a_log, dt_bias, norm_w):
    B, T, _ = qkv_raw.shape
    qkv, new_buf = causal_short_conv(qkv_raw, conv_buf, conv_w)
    q = l2norm(qkv[..., :GDN_QK_W].reshape(B, T, GDN_HEADS, GDN_DK))
    k = l2norm(qkv[..., GDN_QK_W:2 * GDN_QK_W].reshape(B, T, GDN_HEADS, GDN_DK))
    v = qkv[..., 2 * GDN_QK_W:].reshape(B, T, GDN_HEADS, GDN_DV)
    beta = jax.nn.sigmoid(b.astype(jnp.float32))
    g = -jnp.exp(a_log.astype(jnp.float32)) * jax.nn.softplus(a.astype(jnp.float32) + dt_bias.astype(jnp.float32))
    o, S_new = gated_delta_chunked(q, k, v, g, beta, S0)
    o = rmsnorm(o.astype(z.dtype), norm_w) * jax.nn.silu(z.reshape(B, T, GDN_HEADS, GDN_DV))
    return o.reshape(B, T, GDN_V_W), S_new, new_buf


def mla_project(cq, ckv, kr, pos, q_a_norm, w_q_b, kv_a_norm, w_kv_b):
    B, T, _ = cq.shape
    q = (rmsnorm(cq, q_a_norm) @ w_q_b).reshape(B, T, MLA_HEADS, MLA_NOPE + MLA_ROPE)
    cos, sin = rope_tables(pos)
    q_rope = apply_rope(q[..., MLA_NOPE:], cos[:, None, :], sin[:, None, :])
    latent = rmsnorm(ckv, kv_a_norm)
    k_rope = apply_rope(kr, cos, sin)
    w_kv = w_kv_b.reshape(MLA_KV_LORA, MLA_HEADS, MLA_NOPE + MLA_V)
    q_lat = jnp.einsum('bthd,rhd->bthr', q[..., :MLA_NOPE], w_kv[..., :MLA_NOPE])
    return q_lat, q_rope, latent, k_rope, w_kv[..., MLA_NOPE:]


def mla_attend_prompt(q_lat, q_rope, latent, k_rope):
    B, S, H, R = q_lat.shape
    nb = S // Q_BLOCK
    qlb = q_lat.reshape(B, nb, Q_BLOCK, H, R).swapaxes(0, 1)
    qrb = q_rope.reshape(B, nb, Q_BLOCK, H, MLA_ROPE).swapaxes(0, 1)
    key_pos = jnp.arange(S)

    def block(args):
        ql, qr, i = args
        s = jnp.einsum('bqhr,bkr->bhqk', ql, latent) + jnp.einsum('bqhe,bke->bhqk', qr, k_rope)
        s = s.astype(jnp.float32) * MLA_SCALE
        q_pos = i * Q_BLOCK + jnp.arange(Q_BLOCK)
        s = jnp.where(key_pos[None, :] <= q_pos[:, None], s, -jnp.inf)
        p = jax.nn.softmax(s, axis=-1).astype(latent.dtype)
        return jnp.einsum('bhqk,bkr->bqhr', p, latent)

    o = lax.map(block, (qlb, qrb, jnp.arange(nb)))
    return o.swapaxes(0, 1).reshape(B, S, H, R)


def mla_attend_cached(q_lat, q_rope, latent, k_rope, past_latent, past_krope):
    T = q_lat.shape[1]
    s_past = (jnp.einsum('bqhr,bkr->bhqk', q_lat, past_latent)
              + jnp.einsum('bqhe,bke->bhqk', q_rope, past_krope)).astype(jnp.float32) * MLA_SCALE
    s_new = (jnp.einsum('bqhr,bkr->bhqk', q_lat, latent)
             + jnp.einsum('bqhe,bke->bhqk', q_rope, k_rope)).astype(jnp.float32) * MLA_SCALE
    s_new = jnp.where(jnp.tril(jnp.ones((T, T), dtype=bool)), s_new, -jnp.inf)
    m = jnp.maximum(jnp.max(s_past, axis=-1, keepdims=True), jnp.max(s_new, axis=-1, keepdims=True))
    e_past = jnp.exp(s_past - m)
    e_new = jnp.exp(s_new - m)
    denom = jnp.sum(e_past, axis=-1) + jnp.sum(e_new, axis=-1)
    o = (jnp.einsum('bhqk,bkr->bqhr', e_past.astype(past_latent.dtype), past_latent)
         + jnp.einsum('bhqk,bkr->bqhr', e_new.astype(latent.dtype), latent))
    return (o.astype(jnp.float32) / denom.transpose(0, 2, 1)[..., None]).astype(latent.dtype)


def token_mixing(h, pos, S0, conv_buf, past, p):
    B, T, _ = h.shape
    proj = h @ p['w_in']
    qkv_raw, z, a, b, cq, ckv, kr, gate_gdn, gate_mla = jnp.split(proj, IN_OFFSETS, axis=-1)
    y_gdn, S_new, buf_new = gated_deltanet(qkv_raw, z, a, b, conv_buf, S0, p['conv_w'], p['a_log'],
                                           p['dt_bias'], p['gdn_norm'])
    q_lat, q_rope, latent, k_rope, w_uv = mla_project(cq, ckv, kr, pos, p['q_a_norm'], p['w_q_b'],
                                                      p['kv_a_norm'], p['w_kv_b'])
    if past is None:
        o_lat = mla_attend_prompt(q_lat, q_rope, latent, k_rope)
    else:
        o_lat = mla_attend_cached(q_lat, q_rope, latent, k_rope, past[0], past[1])
    y_mla = jnp.einsum('bthr,rhv->bthv', o_lat, w_uv).reshape(B, T, MLA_HEADS * MLA_V)
    merged = (jax.nn.sigmoid(gate_gdn) * (y_gdn @ p['w_o_gdn'])
              + jax.nn.sigmoid(gate_mla) * (y_mla @ p['w_o_mla']))
    return merged @ p['w_out'], S_new, buf_new, latent, k_rope


def decoder_layer(x, c, pos, S0, conv_buf, past, p):
    mod = jax.nn.silu(c) @ p['w_ada'] + p['b_ada']
    sh_a, sc_a, gt_a, sh_m, sc_m, gt_m = jnp.split(mod, N_MOD, axis=-1)
    g = p['norm_g']
    h = rmsnorm(x, g[0]) * (1.0 + sc_a[:, None, :]) + sh_a[:, None, :]
    mix, S_new, buf_new, latent, k_rope = token_mixing(h, pos, S0, conv_buf, past, p)
    x = x + gt_a[:, None, :] * rmsnorm(mix, g[1])
    h = rmsnorm(x, g[2]) * (1.0 + sc_m[:, None, :]) + sh_m[:, None, :]
    f = jnp.square(jax.nn.relu(h @ p['w_up'])) @ p['w_down']
    x = x + gt_m[:, None, :] * rmsnorm(f, g[3])
    return x, S_new, buf_new, latent, k_rope


def setup_inputs(seed: int = 0) -> dict:
    key = jax.random.key(seed)
    ks = jax.random.split(key, 28)
    n_pages = PAST_LEN // PAGE_SIZE
    n_pool = (DEC_BATCH * n_pages * 5) // 4

    def nrm(k, shape, scale):
        return jax.random.normal(k, shape, jnp.float32) * scale

    dt = jnp.exp(jax.random.uniform(ks[12], (DEPTH, GDN_HEADS), jnp.float32, np.log(0.001), np.log(0.1)))
    page_table = jax.random.permutation(ks[6], n_pool)[:DEC_BATCH * n_pages].reshape(DEC_BATCH, n_pages).astype(jnp.int32)
    return {
        'x_prompt': nrm(ks[0], (BATCH, SEQ, D_MODEL), 1.0),
        'x_sample': nrm(ks[1], (DEC_BATCH, DEC_SEQ, D_MODEL), 1.0),
        'cache_mla_latent': nrm(ks[2], (DEPTH, n_pool, PAGE_SIZE, MLA_KV_LORA), 1.0),
        'cache_mla_krope': nrm(ks[3], (DEPTH, n_pool, PAGE_SIZE, MLA_ROPE), 1.0),
        'state_gdn': nrm(ks[4], (DEPTH, DEC_BATCH, GDN_HEADS, GDN_DK, GDN_DV), 0.1),
        'state_conv': nrm(ks[5], (DEPTH, DEC_BATCH, GDN_CONV - 1, CONV_DIM), 1.0),
        'page_table': page_table,
        'c_prompt': nrm(ks[7], (BATCH, D_MODEL), 1.0),
        'c_sample': nrm(ks[8], (DEC_BATCH, D_MODEL), 1.0),
        'w_in': nrm(ks[9], (DEPTH, D_MODEL, W_IN_COLS), D_MODEL ** -0.5),
        'conv_w': nrm(ks[10], (DEPTH, GDN_CONV, CONV_DIM), GDN_CONV ** -0.5),
        'a_log': jnp.log(jax.random.uniform(ks[11], (DEPTH, GDN_HEADS), jnp.float32, 1.0, 16.0)),
        'dt_bias': dt + jnp.log(-jnp.expm1(-dt)),
        'gdn_norm': 1.0 + nrm(ks[13], (DEPTH, GDN_DV), 0.1),
        'q_a_norm': 1.0 + nrm(ks[14], (DEPTH, MLA_Q_LORA), 0.1),
        'w_q_b': nrm(ks[15], (DEPTH, MLA_Q_LORA, MLA_HEADS * (MLA_NOPE + MLA_ROPE)), MLA_Q_LORA ** -0.5),
        'kv_a_norm': 1.0 + nrm(ks[16], (DEPTH, MLA_KV_LORA), 0.1),
        'w_kv_b': nrm(ks[17], (DEPTH, MLA_KV_LORA, MLA_HEADS * (MLA_NOPE + MLA_V)), MLA_KV_LORA ** -0.5),
        'w_o_gdn': nrm(ks[18], (DEPTH, GDN_V_W, D_MODEL), GDN_V_W ** -0.5),
        'w_o_mla': nrm(ks[19], (DEPTH, MLA_HEADS * MLA_V, D_MODEL), (MLA_HEADS * MLA_V) ** -0.5),
        'w_out': nrm(ks[20], (DEPTH, D_MODEL, D_MODEL), D_MODEL ** -0.5),
        'w_up': nrm(ks[21], (DEPTH, D_MODEL, D_FF), D_MODEL ** -0.5),
        'w_down': nrm(ks[22], (DEPTH, D_FF, D_MODEL), D_FF ** -0.5),
        'w_ada': nrm(ks[23], (DEPTH, D_MODEL, N_MOD * D_MODEL), 0.5 * D_MODEL ** -0.5),
        'b_ada': nrm(ks[24], (DEPTH, N_MOD * D_MODEL), 0.01),
        'norm_g': 1.0 + nrm(ks[25], (DEPTH, 4, D_MODEL), 0.1),
    }


def reference(x_prompt, x_sample, cache_mla_latent, cache_mla_krope, state_gdn, state_conv, page_table,
              c_prompt, c_sample, w_in, conv_w, a_log, dt_bias, gdn_norm, q_a_norm, w_q_b, kv_a_norm, w_kv_b,
              w_o_gdn, w_o_mla, w_out, w_up, w_down, w_ada, b_ada, norm_g):
    B, S = x_prompt.shape[0], x_prompt.shape[1]
    DB, T = x_sample.shape[0], x_sample.shape[1]
    n_pages = page_table.shape[1]
    past_len = n_pages * PAGE_SIZE
    pos_p = jnp.arange(S, dtype=jnp.int32)
    pos_s = past_len + jnp.arange(T, dtype=jnp.int32)
    xp, xs = x_prompt, x_sample
    lat_p, kr_p, gdn_p, conv_p = [], [], [], []
    lat_s, kr_s, gdn_s, conv_s = [], [], [], []
    for l in range(DEPTH):
        p = {'w_in': w_in[l], 'conv_w': conv_w[l], 'a_log': a_log[l], 'dt_bias': dt_bias[l],
             'gdn_norm': gdn_norm[l], 'q_a_norm': q_a_norm[l], 'w_q_b': w_q_b[l], 'kv_a_norm': kv_a_norm[l],
             'w_kv_b': w_kv_b[l], 'w_o_gdn': w_o_gdn[l], 'w_o_mla': w_o_mla[l], 'w_out': w_out[l],
             'w_up': w_up[l], 'w_down': w_down[l], 'w_ada': w_ada[l], 'b_ada': b_ada[l], 'norm_g': norm_g[l]}
        S0p = jnp.zeros((B, GDN_HEADS, GDN_DK, GDN_DV), jnp.float32)
        buf0p = jnp.zeros((B, GDN_CONV - 1, CONV_DIM), x_prompt.dtype)
        xp, Sp, bp, lp, kp = decoder_layer(xp, c_prompt, pos_p, S0p, buf0p, None, p)
        lat_p.append(lp)
        kr_p.append(kp)
        gdn_p.append(Sp)
        conv_p.append(bp)
        past_latent = cache_mla_latent[l, page_table].reshape(DB, past_len, MLA_KV_LORA)
        past_krope = cache_mla_krope[l, page_table].reshape(DB, past_len, MLA_ROPE)
        xs, Ss, bs, ls, ksr = decoder_layer(xs, c_sample, pos_s, state_gdn[l], state_conv[l],
                                            (past_latent, past_krope), p)
        lat_s.append(ls)
        kr_s.append(ksr)
        gdn_s.append(Ss)
        conv_s.append(bs)
    return (xp, xs, jnp.stack(lat_p), jnp.stack(kr_p), jnp.stack(gdn_p), jnp.stack(conv_p),
            jnp.stack(lat_s), jnp.stack(kr_s), jnp.stack(gdn_s), jnp.stack(conv_s))
```

```python
import functools

import numpy as np
import jax
import jax.numpy as jnp
from jax import lax
from jax.experimental import pallas as pl
from jax.experimental.pallas import tpu as pltpu

F32 = jnp.float32
BF16 = jnp.bfloat16

GDN_HEADS = 16
GDN_DK = 128
GDN_DV = 128
GDN_CONV = 4
GDN_CHUNK = 64
GDN_QK_W = GDN_HEADS * GDN_DK
GDN_V_W = GDN_HEADS * GDN_DV
CONV_DIM = 2 * GDN_QK_W + GDN_V_W
MLA_HEADS = 16
MLA_Q_LORA = 768
MLA_KV_LORA = 256
MLA_NOPE = 128
MLA_ROPE = 64
MLA_V = 128
MLA_SCALE = (MLA_NOPE + MLA_ROPE) ** -0.5
ROPE_THETA = 10000.0
PAGE_SIZE = 128
N_MOD = 6
EPS = 1e-6

LANE = 128
SUBLANE = 8
VMEM_LIMIT = 56 * 1024 * 1024

SM_CQ = 0
SM_CKV = MLA_Q_LORA
SM_KR = SM_CKV + MLA_KV_LORA
SM_KRS = SM_KR + LANE
SM_AB = SM_KRS + LANE
SM_W = SM_AB + LANE


def _cparams(*sem):
    return pltpu.CompilerParams(dimension_semantics=sem, vmem_limit_bytes=VMEM_LIMIT)


def _sigmoid(x):
    return 1.0 / (1.0 + jnp.exp(-x))


def _bdot(a, b):
    return jnp.dot(a.astype(BF16), b.astype(BF16), preferred_element_type=F32)


_NT = (((1,), (1,)), ((), ()))
_TN = (((0,), (0,)), ((), ()))


def _ada_kernel(c_ref, w_ref, b_ref, o_ref):
    c = c_ref[...]
    a = (c * _sigmoid(c)).astype(BF16)
    o_ref[...] = jnp.dot(a, w_ref[...].astype(BF16), preferred_element_type=F32) + b_ref[...]


def ada_mod(c, w_ada, b_ada, tn=512):
    depth, d, n = w_ada.shape
    mc = c.shape[0]
    return pl.pallas_call(
        _ada_kernel,
        grid=(depth, n // tn),
        in_specs=[pl.BlockSpec((mc, d), lambda l, j: (0, 0)),
                  pl.BlockSpec((None, d, tn), lambda l, j: (l, 0, j)),
                  pl.BlockSpec((None, 1, tn), lambda l, j: (l, 0, j))],
        out_specs=pl.BlockSpec((None, mc, tn), lambda l, j: (l, 0, j)),
        out_shape=jax.ShapeDtypeStruct((depth, mc, n), F32),
        compiler_params=_cparams("parallel", "parallel"),
        name="ada_mod",
    )(c, w_ada, b_ada.reshape(depth, 1, n))


def _rms(x, g):
    return x * lax.rsqrt(jnp.mean(x * x, axis=-1, keepdims=True) + EPS) * g


def _normmod_kernel(x_ref, g_ref, sc_ref, sh_ref, h_ref, *, gi):
    y = _rms(x_ref[...], g_ref[gi:gi + 1, :])
    h_ref[...] = (y * (1.0 + sc_ref[...]) + sh_ref[...]).astype(h_ref.dtype)


def _row_blocks(bsz, t, rows):
    if t >= rows:
        return 1, rows
    return rows // t, t


def norm_mod(x, g4, gi, sc, sh, rows=256):
    bsz, t, d = x.shape
    bb, tt = _row_blocks(bsz, t, rows)
    xs = pl.BlockSpec((bb, tt, d), lambda b, i: (b, i, 0))
    ms = pl.BlockSpec((bb, 1, d), lambda b, i: (b, 0, 0))
    return pl.pallas_call(
        functools.partial(_normmod_kernel, gi=gi),
        grid=(bsz // bb, t // tt),
        in_specs=[xs, pl.BlockSpec(g4.shape, lambda b, i: (0, 0)), ms, ms],
        out_specs=xs,
        out_shape=jax.ShapeDtypeStruct(x.shape, BF16),
        compiler_params=_cparams("parallel", "parallel"),
        name="norm_mod",
    )(x, g4, sc, sh)


def _resid_kernel(x_ref, y_ref, gpost_ref, gt_ref, *rest, gi_post, gi_pre):
    xn = x_ref[...] + gt_ref[...] * _rms(y_ref[...], gpost_ref[gi_post:gi_post + 1, :])
    if gi_pre is None:
        (xo_ref,) = rest
        xo_ref[...] = xn
    else:
        gpre_ref, sc_ref, sh_ref, xo_ref, h_ref = rest
        xo_ref[...] = xn
        h = _rms(xn, gpre_ref[gi_pre:gi_pre + 1, :])
        h_ref[...] = (h * (1.0 + sc_ref[...]) + sh_ref[...]).astype(h_ref.dtype)


def resid_norm(x, y, gpost4, gi_post, gt, pre=None, rows=256):
    bsz, t, d = x.shape
    bb, tt = _row_blocks(bsz, t, rows)
    xs = pl.BlockSpec((bb, tt, d), lambda b, i: (b, i, 0))
    ms = pl.BlockSpec((bb, 1, d), lambda b, i: (b, 0, 0))
    gs = pl.BlockSpec(gpost4.shape, lambda b, i: (0, 0))
    ins = [x, y, gpost4, gt]
    in_specs = [xs, xs, gs, ms]
    if pre is None:
        gi_pre = None
        out_shape = jax.ShapeDtypeStruct(x.shape, F32)
        out_specs = xs
    else:
        gpre4, gi_pre, sc, sh = pre
        ins += [gpre4, sc, sh]
        in_specs += [gs, ms, ms]
        out_shape = (jax.ShapeDtypeStruct(x.shape, F32), jax.ShapeDtypeStruct(x.shape, BF16))
        out_specs = (xs, xs)
    return pl.pallas_call(
        functools.partial(_resid_kernel, gi_post=gi_post, gi_pre=gi_pre),
        grid=(bsz // bb, t // tt),
        in_specs=in_specs, out_specs=out_specs, out_shape=out_shape,
        compiler_params=_cparams("parallel", "parallel"),
        name="resid_norm",
    )(*ins)


def _mm_kernel(a_ref, b_ref, o_ref, *, act):
    acc = jnp.dot(a_ref[...], b_ref[...].astype(BF16), preferred_element_type=F32)
    if act == "relu2":
        r = jnp.maximum(acc, 0.0)
        acc = r * r
    elif act == "sigmoid":
        acc = _sigmoid(acc)
    o_ref[...] = acc.astype(o_ref.dtype)


def matmul(a, w, layer, n, *, col0=0, tm=1024, tn=512, out_dtype=F32, act=None, name="matmul"):
    m, k = a.shape
    tm = min(tm, m)
    cb = col0 // tn
    return pl.pallas_call(
        functools.partial(_mm_kernel, act=act),
        grid=(m // tm, n // tn),
        in_specs=[pl.BlockSpec((tm, k), lambda i, j: (i, 0)),
                  pl.BlockSpec((None, k, tn), lambda i, j: (layer, 0, cb + j))],
        out_specs=pl.BlockSpec((tm, tn), lambda i, j: (i, j)),
        out_shape=jax.ShapeDtypeStruct((m, n), out_dtype),
        compiler_params=_cparams("parallel", "parallel"),
        name=name,
    )(a, w)


def _mm_acc_kernel(a_ref, b_ref, o_ref, acc_ref, *, nk):
    kk = pl.program_id(2)

    @pl.when(kk == 0)
    def _():
        acc_ref[...] = jnp.zeros_like(acc_ref)

    acc_ref[...] += jnp.dot(a_ref[...], b_ref[...].astype(BF16), preferred_element_type=F32)

    @pl.when(kk == nk - 1)
    def _():
        o_ref[...] = acc_ref[...].astype(o_ref.dtype)


def matmul_acc(a, w, layer, *, tm=1024, tn=1024, tk=2048, out_dtype=F32, name="matmul_acc"):
    m, k = a.shape
    n = w.shape[-1]
    tm = min(tm, m)
    nk = k // tk
    return pl.pallas_call(
        functools.partial(_mm_acc_kernel, nk=nk),
        grid=(m // tm, n // tn, nk),
        in_specs=[pl.BlockSpec((tm, tk), lambda i, j, q: (i, q)),
                  pl.BlockSpec((None, tk, tn), lambda i, j, q: (layer, q, j))],
        out_specs=pl.BlockSpec((tm, tn), lambda i, j, q: (i, j)),
        out_shape=jax.ShapeDtypeStruct((m, n), out_dtype),
        scratch_shapes=[pltpu.VMEM((tm, tn), F32)],
        compiler_params=_cparams("parallel", "parallel", "arbitrary"),
        name=name,
    )(a, w)


def _merge_kernel(yg_ref, ym_ref, wg_ref, wm_ref, gg_ref, gm_ref, o_ref):
    a = jnp.dot(yg_ref[...], wg_ref[...].astype(BF16), preferred_element_type=F32)
    b = jnp.dot(ym_ref[...], wm_ref[...].astype(BF16), preferred_element_type=F32)
    o_ref[...] = (gg_ref[...].astype(F32) * a + gm_ref[...].astype(F32) * b).astype(o_ref.dtype)


def merge_branches(yg, ym, w_o_gdn, w_o_mla, gates, layer, *, tm=1024, tn=512):
    m, k = yg.shape
    d = w_o_gdn.shape[-1]
    tm = min(tm, m)
    nb = d // tn
    a_spec = pl.BlockSpec((tm, k), lambda i, j: (i, 0))
    w_spec = pl.BlockSpec((None, k, tn), lambda i, j: (layer, 0, j))
    return pl.pallas_call(
        _merge_kernel,
        grid=(m // tm, nb),
        in_specs=[a_spec, a_spec, w_spec, w_spec,
                  pl.BlockSpec((tm, tn), lambda i, j: (i, j)),
                  pl.BlockSpec((tm, tn), lambda i, j: (i, nb + j))],
        out_specs=pl.BlockSpec((tm, tn), lambda i, j: (i, j)),
        out_shape=jax.ShapeDtypeStruct((m, d), BF16),
        compiler_params=_cparams("parallel", "parallel"),
        name="merge_branches",
    )(yg, ym, w_o_gdn, w_o_mla, gates, gates)


def _small_kernel(h_ref, w_ref, qn_ref, kvn_ref, cos_ref, sin_ref, alog_ref, dtb_ref,
                  cqn_ref, lat_ref, latb_ref, kr_ref, kpad_ref, gb_ref):
    s = jnp.dot(h_ref[...], w_ref[...], preferred_element_type=F32)
    cqn_ref[...] = _rms(s[:, SM_CQ:SM_CKV], qn_ref[...]).astype(cqn_ref.dtype)
    lat = _rms(s[:, SM_CKV:SM_KR], kvn_ref[...])
    lat_ref[...] = lat
    latb_ref[...] = lat.astype(latb_ref.dtype)
    kp = s[:, SM_KR:SM_KRS] * cos_ref[...] + s[:, SM_KRS:SM_AB] * sin_ref[...]
    kr_ref[...] = kp[:, :MLA_ROPE]
    kpad_ref[...] = kp.astype(kpad_ref.dtype)
    ab = s[:, SM_AB:SM_W]
    x = ab + dtb_ref[...]
    softplus = jnp.maximum(x, 0.0) + jnp.log(1.0 + jnp.exp(-jnp.abs(x)))
    g = -jnp.exp(alog_ref[...]) * softplus
    lane = lax.broadcasted_iota(jnp.int32, ab.shape, 1)
    gb_ref[...] = jnp.where(lane < GDN_HEADS, g, _sigmoid(ab))


def small_proj(h, w_small, layer, qn, kvn, cos_t, sin_t, alog, dtb, tm=512):
    m, k = h.shape
    tm = min(tm, m)
    nt = cos_t.shape[0] // tm
    row = lambda n: pl.BlockSpec((tm, n), lambda i: (i, 0))
    vec = lambda n: pl.BlockSpec((1, n), lambda i: (0, 0))
    tab = pl.BlockSpec((tm, LANE), lambda i: (i % nt, 0))
    return pl.pallas_call(
        _small_kernel,
        grid=(m // tm,),
        in_specs=[row(k), pl.BlockSpec((None, k, SM_W), lambda i: (layer, 0, 0)),
                  vec(MLA_Q_LORA), vec(MLA_KV_LORA), tab, tab, vec(LANE), vec(LANE)],
        out_specs=(row(MLA_Q_LORA), row(MLA_KV_LORA), row(MLA_KV_LORA), row(MLA_ROPE), row(LANE), row(LANE)),
        out_shape=(jax.ShapeDtypeStruct((m, MLA_Q_LORA), BF16),
                   jax.ShapeDtypeStruct((m, MLA_KV_LORA), F32),
                   jax.ShapeDtypeStruct((m, MLA_KV_LORA), BF16),
                   jax.ShapeDtypeStruct((m, MLA_ROPE), F32),
                   jax.ShapeDtypeStruct((m, LANE), BF16),
                   jax.ShapeDtypeStruct((m, LANE), F32)),
        compiler_params=_cparams("parallel"),
        name="small_proj",
    )(h, w_small, qn, kvn, cos_t, sin_t, alog, dtb)


def _mlaq_kernel(cqn_ref, wq_ref, wk_ref, cos_ref, sin_ref, ql_ref, qr_ref):
    qa = jnp.dot(cqn_ref[...], wq_ref[...], preferred_element_type=F32)
    cos = cos_ref[...]
    sin = sin_ref[...]
    nope_w = MLA_HEADS * MLA_NOPE
    for h in range(MLA_HEADS):
        lo = h * LANE
        ql = jnp.dot(qa[:, lo:lo + LANE].astype(BF16), wk_ref[h], preferred_element_type=F32)
        ql_ref[h] = (ql * MLA_SCALE).astype(ql_ref.dtype)
        r = qa[:, nope_w + lo:nope_w + lo + LANE] * cos + qa[:, 2 * nope_w + lo:2 * nope_w + lo + LANE] * sin
        qr_ref[h] = (r * MLA_SCALE).astype(qr_ref.dtype)


def mla_query(cqn, wq_all, wk_t, layer, cos_t, sin_t, out_dtype, tm=256):
    m, k = cqn.shape
    tm = min(tm, m)
    nt = cos_t.shape[0] // tm
    nq = wq_all.shape[-1]
    tab = pl.BlockSpec((tm, LANE), lambda i: (i % nt, 0))
    return pl.pallas_call(
        _mlaq_kernel,
        grid=(m // tm,),
        in_specs=[pl.BlockSpec((tm, k), lambda i: (i, 0)),
                  pl.BlockSpec((None, k, nq), lambda i: (layer, 0, 0)),
                  pl.BlockSpec((None, MLA_HEADS, MLA_NOPE, MLA_KV_LORA), lambda i: (layer, 0, 0, 0)),
                  tab, tab],
        out_specs=(pl.BlockSpec((MLA_HEADS, tm, MLA_KV_LORA), lambda i: (0, i, 0)),
                   pl.BlockSpec((MLA_HEADS, tm, LANE), lambda i: (0, i, 0))),
        out_shape=(jax.ShapeDtypeStruct((MLA_HEADS, m, MLA_KV_LORA), out_dtype),
                   jax.ShapeDtypeStruct((MLA_HEADS, m, LANE), out_dtype)),
        compiler_params=_cparams("parallel"),
        name="mla_query",
    )(cqn, wq_all, wk_t, cos_t, sin_t)


def _uv_kernel(o_ref, w_ref, y_ref):
    for h in range(MLA_HEADS):
        y = jnp.dot(o_ref[h].astype(BF16), w_ref[h], preferred_element_type=F32)
        y_ref[:, h * MLA_V:(h + 1) * MLA_V] = y.astype(y_ref.dtype)


def mla_value_up(o_lat, wuv, layer, tm=512):
    _, m, r = o_lat.shape
    tm = min(tm, m)
    return pl.pallas_call(
        _uv_kernel,
        grid=(m // tm,),
        in_specs=[pl.BlockSpec((MLA_HEADS, tm, r), lambda i: (0, i, 0)),
                  pl.BlockSpec((None, MLA_HEADS, r, MLA_V), lambda i: (layer, 0, 0, 0))],
        out_specs=pl.BlockSpec((tm, MLA_HEADS * MLA_V), lambda i: (i, 0)),
        out_shape=jax.ShapeDtypeStruct((m, MLA_HEADS * MLA_V), BF16),
        compiler_params=_cparams("parallel"),
        name="mla_value_up",
    )(o_lat, wuv)


NEG = -1e30


def _attn_prompt_kernel(ql_ref, qr_ref, k_ref, kr_ref, o_ref, m_scr, l_scr, acc_scr, *, tq, tk, nk):
    i = pl.program_id(1)
    kb = pl.program_id(2)
    rows = MLA_HEADS * tq
    last = (i * tq + tq - 1) // tk

    @pl.when(kb == 0)
    def _():
        m_scr[...] = jnp.full_like(m_scr, NEG)
        l_scr[...] = jnp.zeros_like(l_scr)
        acc_scr[...] = jnp.zeros_like(acc_scr)

    def step(masked):
        q = ql_ref[...].reshape(rows, MLA_KV_LORA)
        qr = qr_ref[...].reshape(rows, LANE)
        k = k_ref[...]
        s = (lax.dot_general(q, k, _NT, preferred_element_type=F32)
             + lax.dot_general(qr, kr_ref[...], _NT, preferred_element_type=F32))
        if masked:
            qpos = i * tq + (lax.broadcasted_iota(jnp.int32, s.shape, 0) & (tq - 1))
            kpos = kb * tk + lax.broadcasted_iota(jnp.int32, s.shape, 1)
            s = jnp.where(kpos <= qpos, s, NEG)
        m_old = m_scr[...]
        m_new = jnp.maximum(m_old, jnp.max(s, axis=-1, keepdims=True))
        p = jnp.exp(s - m_new)
        alpha = jnp.exp(m_old - m_new)
        l_scr[...] = alpha * l_scr[...] + jnp.sum(p, axis=-1, keepdims=True)
        acc_scr[...] = alpha * acc_scr[...] + jnp.dot(p.astype(BF16), k, preferred_element_type=F32)
        m_scr[...] = m_new

    @pl.when(kb < last)
    def _():
        step(False)

    @pl.when(kb == last)
    def _():
        step(True)

    @pl.when(kb == nk - 1)
    def _():
        o = acc_scr[...] / l_scr[...]
        o_ref[...] = o.reshape(MLA_HEADS, tq, MLA_KV_LORA).astype(o_ref.dtype)


def attn_prompt(q_lat, q_rope, latb, kpad, bsz, t, tq=128, tk=512):
    nq, nk = t // tq, t // tk
    rows = MLA_HEADS * tq
    q4 = q_lat.reshape(MLA_HEADS, bsz, t, MLA_KV_LORA)
    r4 = q_rope.reshape(MLA_HEADS, bsz, t, LANE)
    k3 = latb.reshape(bsz, t, MLA_KV_LORA)
    kr3 = kpad.reshape(bsz, t, LANE)
    kmap = lambda b, i, kb: (b, jnp.minimum(kb, (i * tq + tq - 1) // tk), 0)
    out = pl.pallas_call(
        functools.partial(_attn_prompt_kernel, tq=tq, tk=tk, nk=nk),
        grid=(bsz, nq, nk),
        in_specs=[pl.BlockSpec((MLA_HEADS, None, tq, MLA_KV_LORA), lambda b, i, kb: (0, b, i, 0)),
                  pl.BlockSpec((MLA_HEADS, None, tq, LANE), lambda b, i, kb: (0, b, i, 0)),
                  pl.BlockSpec((None, tk, MLA_KV_LORA), kmap),
                  pl.BlockSpec((None, tk, LANE), kmap)],
        out_specs=pl.BlockSpec((MLA_HEADS, None, tq, MLA_KV_LORA), lambda b, i, kb: (0, b, i, 0)),
        out_shape=jax.ShapeDtypeStruct((MLA_HEADS, bsz, t, MLA_KV_LORA), BF16),
        scratch_shapes=[pltpu.VMEM((rows, 1), F32), pltpu.VMEM((rows, 1), F32),
                        pltpu.VMEM((rows, MLA_KV_LORA), F32)],
        compiler_params=_cparams("parallel", "parallel", "arbitrary"),
        name="attn_prompt",
    )(q4, r4, k3, kr3)
    return out.reshape(MLA_HEADS, bsz * t, MLA_KV_LORA)


def _attn_sample_kernel(pt_ref, ql_ref, qr_ref, *rest, pg, t, ng):
    lat_refs = rest[:pg]
    kr_refs = rest[pg:2 * pg]
    ln_ref, kn_ref, o_ref, m_scr, l_scr, acc_scr = rest[2 * pg:]
    j = pl.program_id(1)
    rows = MLA_HEADS * t

    @pl.when(j == 0)
    def _():
        m_scr[...] = jnp.full_like(m_scr, NEG)
        l_scr[...] = jnp.zeros_like(l_scr)
        acc_scr[...] = jnp.zeros_like(acc_scr)

    q = ql_ref[...].reshape(rows, MLA_KV_LORA).astype(BF16)
    qr = qr_ref[...].reshape(rows, LANE)[:, :MLA_ROPE].astype(BF16)

    def update(s, v):
        m_old = m_scr[...]
        m_new = jnp.maximum(m_old, jnp.max(s, axis=-1, keepdims=True))
        p = jnp.exp(s - m_new)
        alpha = jnp.exp(m_old - m_new)
        l_scr[...] = alpha * l_scr[...] + jnp.sum(p, axis=-1, keepdims=True)
        acc_scr[...] = alpha * acc_scr[...] + jnp.dot(p.astype(BF16), v, preferred_element_type=F32)
        m_scr[...] = m_new

    k = jnp.concatenate([r[...].astype(BF16) for r in lat_refs], axis=0)
    kr = jnp.concatenate([r[...].astype(BF16) for r in kr_refs], axis=0)
    s = (lax.dot_general(q, k, _NT, preferred_element_type=F32)
         + lax.dot_general(qr, kr, _NT, preferred_element_type=F32))
    update(s, k)

    @pl.when(j == ng - 1)
    def _():
        pad = 2 * SUBLANE - t
        kn = jnp.concatenate([ln_ref[...], jnp.zeros((pad, MLA_KV_LORA), F32)], axis=0).astype(BF16)
        krn = jnp.concatenate([kn_ref[...], jnp.zeros((pad, MLA_ROPE), F32)], axis=0).astype(BF16)
        sn = (lax.dot_general(q, kn, _NT, preferred_element_type=F32)
              + lax.dot_general(qr, krn, _NT, preferred_element_type=F32))
        qt = lax.broadcasted_iota(jnp.int32, sn.shape, 0) & (t - 1)
        kt = lax.broadcasted_iota(jnp.int32, sn.shape, 1)
        sn = jnp.where(kt <= qt, sn, NEG)
        update(sn, kn)
        o = acc_scr[...] / l_scr[...]
        o_ref[...] = o.reshape(MLA_HEADS, t, MLA_KV_LORA).astype(o_ref.dtype)


def attn_sample(q_lat, q_rope, cache_lat, cache_kr, page_table, lat_new, kr_new, layer, db, t, pg=8):
    n_pages = page_table.shape[1]
    ng = n_pages // pg
    pt = page_table.reshape(-1)
    rows = MLA_HEADS * t

    def page_spec(width, p):
        return pl.BlockSpec((None, None, PAGE_SIZE, width),
                            lambda b, j, pt_ref: (layer, pt_ref[b * n_pages + j * pg + p], 0, 0))

    grid_spec = pltpu.PrefetchScalarGridSpec(
        num_scalar_prefetch=1,
        grid=(db, ng),
        in_specs=([pl.BlockSpec((MLA_HEADS, t, MLA_KV_LORA), lambda b, j, pt_ref: (0, b, 0)),
                   pl.BlockSpec((MLA_HEADS, t, LANE), lambda b, j, pt_ref: (0, b, 0))]
                  + [page_spec(MLA_KV_LORA, p) for p in range(pg)]
                  + [page_spec(MLA_ROPE, p) for p in range(pg)]
                  + [pl.BlockSpec((t, MLA_KV_LORA), lambda b, j, pt_ref: (b, 0)),
                     pl.BlockSpec((t, MLA_ROPE), lambda b, j, pt_ref: (b, 0))]),
        out_specs=pl.BlockSpec((MLA_HEADS, t, MLA_KV_LORA), lambda b, j, pt_ref: (0, b, 0)),
        scratch_shapes=[pltpu.VMEM((rows, 1), F32), pltpu.VMEM((rows, 1), F32),
                        pltpu.VMEM((rows, MLA_KV_LORA), F32)],
    )
    return pl.pallas_call(
        functools.partial(_attn_sample_kernel, pg=pg, t=t, ng=ng),
        grid_spec=grid_spec,
        out_shape=jax.ShapeDtypeStruct((MLA_HEADS, db * t, MLA_KV_LORA), F32),
        compiler_params=_cparams("parallel", "arbitrary"),
        name="attn_sample",
    )(pt, q_lat, q_rope, *([cache_lat] * pg), *([cache_kr] * pg), lat_new, kr_new)


def _cumsum_rows(x, seg):
    row = lax.broadcasted_iota(jnp.int32, x.shape, 0) & (seg - 1)
    s = 1
    while s < seg:
        x = x + jnp.where(row >= s, pltpu.roll(x, s, 0), 0.0)
        s *= 2
    return x


def _col(x, lane):
    li = lax.broadcasted_iota(jnp.int32, x.shape, 1)
    return jnp.sum(jnp.where(li == lane, x, 0.0), axis=1, keepdims=True)


def _rowvec(xt, sub):
    si = lax.broadcasted_iota(jnp.int32, xt.shape, 0)
    return jnp.sum(jnp.where(si == sub, xt, 0.0), axis=0, keepdims=True)


def _transpose_block(x):
    c = x.shape[0]
    if c < LANE:
        x = jnp.concatenate([x, jnp.zeros((LANE - c, LANE), x.dtype)], axis=0)
    return x.T[:, :c]


def _l2norm_heads(x, nh):
    outs = []
    for j in range(nh):
        xh = x[:, j * LANE:(j + 1) * LANE]
        outs.append(xh * lax.rsqrt(jnp.sum(xh * xh, axis=-1, keepdims=True) + EPS))
    return outs


def _chunk_masks(c, same, nlevel):
    ri = lax.broadcasted_iota(jnp.int32, (c, c), 0)
    ci = lax.broadcasted_iota(jnp.int32, (c, c), 1)
    tri = ri >= ci
    strict = ri > ci
    if same is not None:
        tri = jnp.logical_and(tri, same)
        strict = jnp.logical_and(strict, same)
    lower = [jnp.logical_and((ri >> k) == (ci >> k) + 1, (ri >> (k + 1)) == (ci >> (k + 1)))
             for k in range(nlevel)]
    return tri, strict, ri == ci, lower


def _chunk_prep(q, k, v, gcol, grow, bcol, glast, masks):
    tri, strict, diag, lower = masks
    qs = q * (GDN_DK ** -0.5)
    kb = k.astype(BF16)
    kk = lax.dot_general(kb, kb, _NT, preferred_element_type=F32)
    qk = lax.dot_general(qs.astype(BF16), kb, _NT, preferred_element_type=F32)
    decay = jnp.where(tri, jnp.exp(jnp.where(tri, gcol - grow, 0.0)), 0.0)
    lmat = jnp.where(strict, bcol * kk * decay, 0.0)
    x = jnp.where(diag, 1.0, 0.0) - jnp.where(lower[0], lmat, 0.0)
    for m in lower[1:]:
        x = x - _bdot(_bdot(x, jnp.where(m, lmat, 0.0)), x)
    eg = jnp.exp(gcol)
    rhs = jnp.concatenate([v * bcol, k * (bcol * eg)], axis=1)
    uw = _bdot(x, rhs)
    u = uw[:, :GDN_DV]
    w = uw[:, GDN_DV:]
    return u, w, qk * decay, qs * eg, k * jnp.exp(glast - gcol)


def _gated_out(o, z, nw):
    y = o * lax.rsqrt(jnp.mean(o * o, axis=-1, keepdims=True) + EPS) * nw
    return y * (z * _sigmoid(z))


def _gdn_prompt_kernel(q_ref, k_ref, v_ref, z_ref, hq_ref, hk_ref, hv_ref, cwq_ref, cwk_ref, cwv_ref,
                       gb_ref, nw_ref, y_ref, so_ref, s_scr, *, hb, rows, nr):
    hg = pl.program_id(1)
    r = pl.program_id(2)
    c = GDN_CHUNK

    @pl.when(r == 0)
    def _():
        s_scr[...] = jnp.zeros_like(s_scr)

    def conv(x_ref, h_ref, w_ref):
        u = x_ref[...]
        halo = jnp.where(r > 0, h_ref[...], 0.0)
        w = w_ref[...]
        acc = u * w[GDN_CONV - 1:GDN_CONV, :]
        row8 = lax.broadcasted_iota(jnp.int32, halo.shape, 0)
        for s in range(1, GDN_CONV):
            ru = pltpu.roll(u, s, 0)
            top = jnp.where(row8 < s, pltpu.roll(halo, s, 0), ru[:SUBLANE])
            sh = jnp.concatenate([top, ru[SUBLANE:]], axis=0)
            acc = acc + sh * w[GDN_CONV - 1 - s:GDN_CONV - s, :]
        return acc * _sigmoid(acc)

    qh = _l2norm_heads(conv(q_ref, hq_ref, cwq_ref), hb)
    kh = _l2norm_heads(conv(k_ref, hk_ref, cwk_ref), hb)
    vv = conv(v_ref, hv_ref, cwv_ref)
    gb = gb_ref[...]
    gc_all = _cumsum_rows(gb, c)
    nw = nw_ref[...]
    masks = _chunk_masks(c, None, c.bit_length() - 1)
    for ch in range(rows // c):
        sl = slice(ch * c, (ch + 1) * c)
        gc = gc_all[sl]
        gct = _transpose_block(gc)
        for j in range(hb):
            head = hg * hb + j
            gcol = _col(gc, head)
            bcol = _col(gb[sl], GDN_HEADS + head)
            grow = _rowvec(gct, head)
            glast = gcol[c - 1:c, :]
            u, w, qkd, qd, kd = _chunk_prep(qh[j][sl], kh[j][sl], vv[sl, j * LANE:(j + 1) * LANE],
                                            gcol, grow, bcol, glast, masks)
            s_old = s_scr[j]
            wq = _bdot(jnp.concatenate([w, qd], axis=0), s_old)
            v_new = u - wq[:c]
            o = wq[c:] + _bdot(qkd, v_new)
            s_scr[j] = s_old * jnp.exp(glast) + lax.dot_general(
                kd.astype(BF16), v_new.astype(BF16), _TN, preferred_element_type=F32)
            y_ref[sl, j * LANE:(j + 1) * LANE] = _gated_out(
                o, z_ref[sl, j * LANE:(j + 1) * LANE], nw).astype(y_ref.dtype)

    @pl.when(r == nr - 1)
    def _():
        so_ref[...] = s_scr[...]


def gdn_prompt(qkvz, gb, conv_w, norm_w, layer, bsz, t, hb=4, rows=256):
    w = hb * LANE
    nr = t // rows
    x3 = qkvz.reshape(bsz, t, -1)
    gb3 = gb.reshape(bsz, t, LANE)
    kq, kv_, kz = GDN_QK_W // w, 2 * GDN_QK_W // w, CONV_DIM // w
    hr = rows // SUBLANE

    def xs(off):
        return pl.BlockSpec((None, rows, w), lambda b, g, r: (b, r, off + g))

    def hs(off):
        return pl.BlockSpec((None, SUBLANE, w), lambda b, g, r: (b, jnp.maximum(r * hr - 1, 0), off + g))

    def ws(off):
        return pl.BlockSpec((None, GDN_CONV, w), lambda b, g, r: (layer, 0, off + g))

    return pl.pallas_call(
        functools.partial(_gdn_prompt_kernel, hb=hb, rows=rows, nr=nr),
        grid=(bsz, GDN_HEADS // hb, nr),
        in_specs=[xs(0), xs(kq), xs(kv_), xs(kz), hs(0), hs(kq), hs(kv_), ws(0), ws(kq), ws(kv_),
                  pl.BlockSpec((None, rows, LANE), lambda b, g, r: (b, r, 0)),
                  pl.BlockSpec((1, GDN_DV), lambda b, g, r: (0, 0))],
        out_specs=(pl.BlockSpec((None, rows, w), lambda b, g, r: (b, r, g)),
                   pl.BlockSpec((None, hb, GDN_DK, GDN_DV), lambda b, g, r: (b, g, 0, 0))),
        out_shape=(jax.ShapeDtypeStruct((bsz, t, GDN_V_W), BF16),
                   jax.ShapeDtypeStruct((bsz, GDN_HEADS, GDN_DK, GDN_DV), F32)),
        scratch_shapes=[pltpu.VMEM((hb, GDN_DK, GDN_DV), F32)],
        compiler_params=_cparams("parallel", "parallel", "arbitrary"),
        name="gdn_prompt",
    )(x3, x3, x3, x3, x3, x3, x3, conv_w, conv_w, conv_w, gb3, norm_w)


def _gdn_sample_kernel(q_ref, k_ref, v_ref, z_ref, hq_ref, hk_ref, hv_ref, cwq_ref, cwk_ref, cwv_ref,
                       gb_ref, nw_ref, s0_ref, y_ref, so_ref, *, hb, t, nb):
    hg = pl.program_id(1)
    c = nb * t

    def conv(x_ref, h_ref, w_ref):
        u = x_ref[...]
        halo = h_ref[...]
        w = w_ref[...]
        acc = u * w[GDN_CONV - 1:GDN_CONV, :]
        tt = lax.broadcasted_iota(jnp.int32, u.shape, 0) & (t - 1)
        for s in range(1, GDN_CONV):
            sh = jnp.where(tt < s, pltpu.roll(halo, c - t + s, 0), pltpu.roll(u, s, 0))
            acc = acc + sh * w[GDN_CONV - 1 - s:GDN_CONV - s, :]
        return acc * _sigmoid(acc)

    qh = _l2norm_heads(conv(q_ref, hq_ref, cwq_ref), hb)
    kh = _l2norm_heads(conv(k_ref, hk_ref, cwk_ref), hb)
    vv = conv(v_ref, hv_ref, cwv_ref)
    gb = gb_ref[...]
    gc = _cumsum_rows(gb, t)
    gl = jnp.broadcast_to(gc.reshape(nb, t, LANE)[:, t - 1:t, :], (nb, t, LANE)).reshape(c, LANE)
    gct = _transpose_block(gc)
    nw = nw_ref[...]
    shift = t.bit_length() - 1
    same = (lax.broadcasted_iota(jnp.int32, (c, c), 0) >> shift) == (lax.broadcasted_iota(jnp.int32, (c, c), 1) >> shift)
    masks = _chunk_masks(c, same, shift)
    rowreq = lax.broadcasted_iota(jnp.int32, (c, LANE), 0) >> shift
    for j in range(hb):
        head = hg * hb + j
        gcol = _col(gc, head)
        bcol = _col(gb, GDN_HEADS + head)
        grow = _rowvec(gct, head)
        glast = _col(gl, head)
        u, w, qkd, qd, kd = _chunk_prep(qh[j], kh[j], vv[:, j * LANE:(j + 1) * LANE],
                                        gcol, grow, bcol, glast, masks)
        s_old = [s0_ref[b, j] for b in range(nb)]
        vn, oi = [], []
        for b in range(nb):
            rs = slice(b * t, (b + 1) * t)
            wq = _bdot(jnp.concatenate([w[rs], qd[rs]], axis=0), s_old[b])
            vn.append(u[rs] - wq[:t])
            oi.append(wq[t:])
        v_new = jnp.concatenate(vn, axis=0)
        o = jnp.concatenate(oi, axis=0) + _bdot(qkd, v_new)
        vb = v_new.astype(BF16)
        for b in range(nb):
            kdb = jnp.where(rowreq == b, kd, 0.0).astype(BF16)
            so_ref[b, j] = s_old[b] * jnp.exp(glast[b * t:b * t + 1, :]) + lax.dot_general(
                kdb, vb, _TN, preferred_element_type=F32)
        y_ref[:, j * LANE:(j + 1) * LANE] = _gated_out(o, z_ref[:, j * LANE:(j + 1) * LANE], nw).astype(y_ref.dtype)


def gdn_sample(qkvz, gb, conv_w, norm_w, halo, state, layer, db, t, hb=4, nb=8):
    w = hb * LANE
    c = nb * t
    kq, kv_, kz = GDN_QK_W // w, 2 * GDN_QK_W // w, CONV_DIM // w

    def xs(off):
        return pl.BlockSpec((c, w), lambda i, g: (i, off + g))

    def ws(off):
        return pl.BlockSpec((None, GDN_CONV, w), lambda i, g: (layer, 0, off + g))

    return pl.pallas_call(
        functools.partial(_gdn_sample_kernel, hb=hb, t=t, nb=nb),
        grid=(db // nb, GDN_HEADS // hb),
        in_specs=[xs(0), xs(kq), xs(kv_), xs(kz), xs(0), xs(kq), xs(kv_), ws(0), ws(kq), ws(kv_),
                  pl.BlockSpec((c, LANE), lambda i, g: (i, 0)),
                  pl.BlockSpec((1, GDN_DV), lambda i, g: (0, 0)),
                  pl.BlockSpec((None, nb, hb, GDN_DK, GDN_DV), lambda i, g: (layer, i, g, 0, 0))],
        out_specs=(pl.BlockSpec((c, w), lambda i, g: (i, g)),
                   pl.BlockSpec((nb, hb, GDN_DK, GDN_DV), lambda i, g: (i, g, 0, 0))),
        out_shape=(jax.ShapeDtypeStruct((db * t, GDN_V_W), BF16),
                   jax.ShapeDtypeStruct((db, GDN_HEADS, GDN_DK, GDN_DV), F32)),
        compiler_params=_cparams("parallel", "parallel"),
        name="gdn_sample",
    )(qkvz, qkvz, qkvz, qkvz, halo, halo, halo, conv_w, conv_w, conv_w, gb, norm_w, state)


def _swap_halves(x):
    half = x.shape[-1] // 2
    return jnp.concatenate([x[..., half:], x[..., :half]], axis=-1)


def _pack_small(w_in):
    depth, d, _ = w_in.shape
    o_a = CONV_DIM + GDN_V_W
    o_b = o_a + GDN_HEADS
    o_cq = o_b + GDN_HEADS
    o_ckv = o_cq + MLA_Q_LORA
    o_kr = o_ckv + MLA_KV_LORA
    o_g = o_kr + MLA_ROPE
    z = lambda n: jnp.zeros((depth, d, n), w_in.dtype)
    kr = w_in[:, :, o_kr:o_g]
    small = jnp.concatenate([
        w_in[:, :, o_cq:o_ckv], w_in[:, :, o_ckv:o_kr],
        kr, z(LANE - MLA_ROPE), _swap_halves(kr), z(LANE - MLA_ROPE),
        w_in[:, :, o_a:o_b], w_in[:, :, o_b:o_cq], z(LANE - 2 * GDN_HEADS)], axis=-1).astype(BF16)
    gates = w_in[:, :, o_g:].astype(BF16)
    return small, gates


def _pack_mla(w_q_b, w_kv_b):
    depth = w_q_b.shape[0]
    wq = w_q_b.reshape(depth, MLA_Q_LORA, MLA_HEADS, MLA_NOPE + MLA_ROPE)
    nope = wq[..., :MLA_NOPE].reshape(depth, MLA_Q_LORA, -1)
    rope = wq[..., MLA_NOPE:]
    pad = jnp.zeros(rope.shape[:-1] + (LANE - MLA_ROPE,), rope.dtype)
    rope_p = jnp.concatenate([rope, pad], axis=-1).reshape(depth, MLA_Q_LORA, -1)
    rope_s = jnp.concatenate([_swap_halves(rope), pad], axis=-1).reshape(depth, MLA_Q_LORA, -1)
    wq_all = jnp.concatenate([nope, rope_p, rope_s], axis=-1).astype(BF16)
    wkv = w_kv_b.reshape(depth, MLA_KV_LORA, MLA_HEADS, MLA_NOPE + MLA_V)
    wk_t = wkv[..., :MLA_NOPE].transpose(0, 2, 3, 1).astype(BF16)
    wuv = wkv[..., MLA_NOPE:].transpose(0, 2, 1, 3).astype(BF16)
    return wq_all, wk_t, wuv


def _rope_tables(pos):
    half = MLA_ROPE // 2
    inv_freq = ROPE_THETA ** (-jnp.arange(half, dtype=F32) / half)
    ang = pos.astype(F32)[:, None] * inv_freq[None, :]
    cos, sin = jnp.cos(ang), jnp.sin(ang)
    z = jnp.zeros((pos.shape[0], LANE - MLA_ROPE), F32)
    return jnp.concatenate([cos, cos, z], axis=-1), jnp.concatenate([-sin, sin, z], axis=-1)


def _lane_pad(v, n=LANE):
    return jnp.concatenate([v, jnp.zeros((n - v.shape[0],), v.dtype)]).reshape(1, n)


def _layer(x, h, mods, l, wts, tabs, group):
    bsz, t, d = x.shape
    m = bsz * t
    sh_a, sc_a, gt_a, sh_m, sc_m, gt_m = mods[l]
    h2 = h.reshape(m, d)
    qkvz = matmul(h2, wts["w_in"], l, CONV_DIM + GDN_V_W, name="in_proj_gdn")
    gates = matmul(h2, wts["w_gates"], l, 2 * d, tn=1024, out_dtype=BF16, act="sigmoid", name="in_proj_gates")
    cos_t, sin_t = tabs
    cqn, lat, latb, kr, kpad, gb = small_proj(
        h2, wts["w_small"], l, wts["q_a_norm"][l:l + 1], wts["kv_a_norm"][l:l + 1], cos_t, sin_t,
        wts["alog"][l], wts["dtb"][l])
    norm_w = wts["gdn_norm"][l:l + 1]
    if group["kind"] == "prompt":
        y_gdn, s_new = gdn_prompt(qkvz, gb, wts["conv_w"], norm_w, l, bsz, t)
        y_gdn = y_gdn.reshape(m, GDN_V_W)
        q_lat, q_rope = mla_query(cqn, wts["wq_all"], wts["wk_t"], l, cos_t, sin_t, BF16)
        o_lat = attn_prompt(q_lat, q_rope, latb, kpad, bsz, t)
    else:
        y_gdn, s_new = gdn_sample(qkvz, gb, wts["conv_w"], norm_w, group["halo"][l], group["state"], l, bsz, t)
        q_lat, q_rope = mla_query(cqn, wts["wq_all"], wts["wk_t"], l, cos_t, sin_t, F32)
        o_lat = attn_sample(q_lat, q_rope, group["cache_lat"], group["cache_kr"], group["page_table"],
                            lat, kr, l, bsz, t)
    y_mla = mla_value_up(o_lat, wts["wuv"], l)
    merged = merge_branches(y_gdn, y_mla, wts["w_o_gdn"], wts["w_o_mla"], gates, l)
    mix = matmul(merged, wts["w_out"], l, d, name="out_proj").reshape(bsz, t, d)
    g4 = wts["norm_g"][l]
    x, hm = resid_norm(x, mix, g4, 1, gt_a, pre=(g4, 2, sc_m, sh_m))
    f = matmul(hm.reshape(m, d), wts["w_up"], l, wts["w_up"].shape[-1], out_dtype=BF16, act="relu2", name="ffn_up")
    f = matmul_acc(f, wts["w_down"], l, name="ffn_down").reshape(bsz, t, d)
    if l + 1 < len(mods):
        nsh, nsc = mods[l + 1][0], mods[l + 1][1]
        x, h_next = resid_norm(x, f, g4, 3, gt_m, pre=(wts["norm_g"][l + 1], 0, nsc, nsh))
    else:
        x, h_next = resid_norm(x, f, g4, 3, gt_m), None
    conv_new = qkvz.reshape(bsz, t, -1)[:, t - (GDN_CONV - 1):, :CONV_DIM]
    return x, h_next, (lat.reshape(bsz, t, -1), kr.reshape(bsz, t, -1), s_new, conv_new)


def kernel(x_prompt, x_sample, cache_mla_latent, cache_mla_krope, state_gdn, state_conv, page_table, c_prompt, c_sample, w_in, conv_w, a_log, dt_bias, gdn_norm, q_a_norm, w_q_b, kv_a_norm, w_kv_b, w_o_gdn, w_o_mla, w_out, w_up, w_down, w_ada, b_ada, norm_g):
    bsz, seq, d = x_prompt.shape
    db, t = x_sample.shape[0], x_sample.shape[1]
    depth = w_in.shape[0]
    past_len = page_table.shape[1] * PAGE_SIZE

    w_small, w_gates = _pack_small(w_in)
    wq_all, wk_t, wuv = _pack_mla(w_q_b, w_kv_b)
    wts = dict(w_in=w_in, w_gates=w_gates, w_small=w_small, q_a_norm=q_a_norm, kv_a_norm=kv_a_norm,
               alog=jnp.stack([_lane_pad(a_log[l]) for l in range(depth)]),
               dtb=jnp.stack([_lane_pad(dt_bias[l]) for l in range(depth)]),
               gdn_norm=gdn_norm, conv_w=conv_w, wq_all=wq_all, wk_t=wk_t, wuv=wuv,
               w_o_gdn=w_o_gdn, w_o_mla=w_o_mla, w_out=w_out, w_up=w_up, w_down=w_down, norm_g=norm_g)

    tabs_p = _rope_tables(jnp.arange(seq, dtype=jnp.int32))
    rows_s = min(512, db * t)
    tabs_s = _rope_tables(past_len + (jnp.arange(rows_s, dtype=jnp.int32) % t))

    nc = bsz + db
    mc = -(-nc // 16) * 16
    c_all = jnp.concatenate([c_prompt, c_sample, jnp.zeros((mc - nc, d), F32)], axis=0)
    mod = ada_mod(c_all, w_ada, b_ada)

    def split_mods(lo, n):
        return [tuple(mod[l, lo:lo + n, i * d:(i + 1) * d].reshape(n, 1, d) for i in range(N_MOD))
                for l in range(depth)]

    mods_p = split_mods(0, bsz)
    mods_s = split_mods(bsz, db)

    halo = jnp.pad(state_conv, ((0, 0), (0, 0), (t - (GDN_CONV - 1), 0), (0, 0))).reshape(depth, db * t, CONV_DIM)
    grp_p = dict(kind="prompt")
    grp_s = dict(kind="sample", halo=halo, state=state_gdn, cache_lat=cache_mla_latent,
                 cache_kr=cache_mla_krope, page_table=page_table)

    xp, xs = x_prompt, x_sample
    hp = norm_mod(xp, norm_g[0], 0, mods_p[0][1], mods_p[0][0])
    hs = norm_mod(xs, norm_g[0], 0, mods_s[0][1], mods_s[0][0])
    leaves_p, leaves_s = [], []
    for l in range(depth):
        xp, hp, lp = _layer(xp, hp, mods_p, l, wts, tabs_p, grp_p)
        xs, hs, ls = _layer(xs, hs, mods_s, l, wts, tabs_s, grp_s)
        leaves_p.append(lp)
        leaves_s.append(ls)
    stack = lambda leaves, i: jnp.stack([lv[i] for lv in leaves])
    return (xp, xs, stack(leaves_p, 0), stack(leaves_p, 1), stack(leaves_p, 2), stack(leaves_p, 3),
            stack(leaves_s, 0), stack(leaves_s, 1), stack(leaves_s, 2), stack(leaves_s, 3))
```

```python
import functools

import numpy as np
import jax
import jax.numpy as jnp
from jax import lax
from jax.experimental import pallas as pl
from jax.experimental.pallas import tpu as pltpu

F32 = jnp.float32
BF16 = jnp.bfloat16

GDN_HEADS = 16
GDN_DK = 128
GDN_DV = 128
GDN_CONV = 4
GDN_CHUNK = 64
GDN_QK_W = GDN_HEADS * GDN_DK
GDN_V_W = GDN_HEADS * GDN_DV
CONV_DIM = 2 * GDN_QK_W + GDN_V_W
MLA_HEADS = 16
MLA_Q_LORA = 768
MLA_KV_LORA = 256
MLA_NOPE = 128
MLA_ROPE = 64
MLA_V = 128
MLA_SCALE = (MLA_NOPE + MLA_ROPE) ** -0.5
ROPE_THETA = 10000.0
PAGE_SIZE = 128
N_MOD = 6
EPS = 1e-6

LANE = 128
SUBLANE = 8
VMEM_LIMIT = 56 * 1024 * 1024

SM_CQ = 0
SM_CKV = MLA_Q_LORA
SM_KR = SM_CKV + MLA_KV_LORA
SM_KRS = SM_KR + LANE
SM_AB = SM_KRS + LANE
SM_W = SM_AB + LANE


def _cparams(*sem):
    return pltpu.CompilerParams(dimension_semantics=sem, vmem_limit_bytes=VMEM_LIMIT)


def _sigmoid(x):
    return 1.0 / (1.0 + jnp.exp(-x))


def _bdot(a, b):
    return jnp.dot(a.astype(BF16), b.astype(BF16), preferred_element_type=F32)


_NT = (((1,), (1,)), ((), ()))
_TN = (((0,), (0,)), ((), ()))


def _ada_kernel(c_ref, w_ref, b_ref, o_ref):
    c = c_ref[...]
    a = (c * _sigmoid(c)).astype(BF16)
    o_ref[...] = jnp.dot(a, w_ref[...].astype(BF16), preferred_element_type=F32) + b_ref[...]


def ada_mod(c, w_ada, b_ada, tn=512):
    depth, d, n = w_ada.shape
    mc = c.shape[0]
    return pl.pallas_call(
        _ada_kernel,
        grid=(depth, n // tn),
        in_specs=[pl.BlockSpec((mc, d), lambda l, j: (0, 0)),
                  pl.BlockSpec((None, d, tn), lambda l, j: (l, 0, j)),
                  pl.BlockSpec((None, 1, tn), lambda l, j: (l, 0, j))],
        out_specs=pl.BlockSpec((None, mc, tn), lambda l, j: (l, 0, j)),
        out_shape=jax.ShapeDtypeStruct((depth, mc, n), F32),
        compiler_params=_cparams("parallel", "parallel"),
        name="ada_mod",
    )(c, w_ada, b_ada.reshape(depth, 1, n))


def _rms(x, g):
    return x * lax.rsqrt(jnp.mean(x * x, axis=-1, keepdims=True) + EPS) * g


def _normmod_kernel(x_ref, g_ref, sc_ref, sh_ref, h_ref, *, gi):
    y = _rms(x_ref[...], g_ref[gi:gi + 1, :])
    h_ref[...] = (y * (1.0 + sc_ref[...]) + sh_ref[...]).astype(h_ref.dtype)


def _row_blocks(bsz, t, rows):
    if t >= rows:
        return 1, rows
    return rows // t, t


def norm_mod(x, g4, gi, sc, sh, rows=256):
    bsz, t, d = x.shape
    bb, tt = _row_blocks(bsz, t, rows)
    xs = pl.BlockSpec((bb, tt, d), lambda b, i: (b, i, 0))
    ms = pl.BlockSpec((bb, 1, d), lambda b, i: (b, 0, 0))
    return pl.pallas_call(
        functools.partial(_normmod_kernel, gi=gi),
        grid=(bsz // bb, t // tt),
        in_specs=[xs, pl.BlockSpec(g4.shape, lambda b, i: (0, 0)), ms, ms],
        out_specs=xs,
        out_shape=jax.ShapeDtypeStruct(x.shape, BF16),
        compiler_params=_cparams("parallel", "parallel"),
        name="norm_mod",
    )(x, g4, sc, sh)


def _resid_kernel(x_ref, y_ref, gpost_ref, gt_ref, *rest, gi_post, gi_pre):
    xn = x_ref[...] + gt_ref[...] * _rms(y_ref[...], gpost_ref[gi_post:gi_post + 1, :])
    if gi_pre is None:
        (xo_ref,) = rest
        xo_ref[...] = xn
    else:
        gpre_ref, sc_ref, sh_ref, xo_ref, h_ref = rest
        xo_ref[...] = xn
        h = _rms(xn, gpre_ref[gi_pre:gi_pre + 1, :])
        h_ref[...] = (h * (1.0 + sc_ref[...]) + sh_ref[...]).astype(h_ref.dtype)


def resid_norm(x, y, gpost4, gi_post, gt, pre=None, rows=256):
    bsz, t, d = x.shape
    bb, tt = _row_blocks(bsz, t, rows)
    xs = pl.BlockSpec((bb, tt, d), lambda b, i: (b, i, 0))
    ms = pl.BlockSpec((bb, 1, d), lambda b, i: (b, 0, 0))
    gs = pl.BlockSpec(gpost4.shape, lambda b, i: (0, 0))
    ins = [x, y, gpost4, gt]
    in_specs = [xs, xs, gs, ms]
    if pre is None:
        gi_pre = None
        out_shape = jax.ShapeDtypeStruct(x.shape, F32)
        out_specs = xs
    else:
        gpre4, gi_pre, sc, sh = pre
        ins += [gpre4, sc, sh]
        in_specs += [gs, ms, ms]
        out_shape = (jax.ShapeDtypeStruct(x.shape, F32), jax.ShapeDtypeStruct(x.shape, BF16))
        out_specs = (xs, xs)
    return pl.pallas_call(
        functools.partial(_resid_kernel, gi_post=gi_post, gi_pre=gi_pre),
        grid=(bsz // bb, t // tt),
        in_specs=in_specs, out_specs=out_specs, out_shape=out_shape,
        compiler_params=_cparams("parallel", "parallel"),
        name="resid_norm",
    )(*ins)


def _mm_kernel(a_ref, b_ref, o_ref, *, act):
    acc = jnp.dot(a_ref[...], b_ref[...].astype(BF16), preferred_element_type=F32)
    if act == "relu2":
        r = jnp.maximum(acc, 0.0)
        acc = r * r
    elif act == "sigmoid":
        acc = _sigmoid(acc)
    o_ref[...] = acc.astype(o_ref.dtype)


def matmul(a, w, layer, n, *, col0=0, tm=1024, tn=512, out_dtype=F32, act=None, name="matmul"):
    m, k = a.shape
    tm = min(tm, m)
    cb = col0 // tn
    return pl.pallas_call(
        functools.partial(_mm_kernel, act=act),
        grid=(m // tm, n // tn),
        in_specs=[pl.BlockSpec((tm, k), lambda i, j: (i, 0)),
                  pl.BlockSpec((None, k, tn), lambda i, j: (layer, 0, cb + j))],
        out_specs=pl.BlockSpec((tm, tn), lambda i, j: (i, j)),
        out_shape=jax.ShapeDtypeStruct((m, n), out_dtype),
        compiler_params=_cparams("parallel", "parallel"),
        name=name,
    )(a, w)


def _mm_acc_kernel(a_ref, b_ref, o_ref, acc_ref, *, nk):
    kk = pl.program_id(2)

    @pl.when(kk == 0)
    def _():
        acc_ref[...] = jnp.zeros_like(acc_ref)

    acc_ref[...] += jnp.dot(a_ref[...], b_ref[...].astype(BF16), preferred_element_type=F32)

    @pl.when(kk == nk - 1)
    def _():
        o_ref[...] = acc_ref[...].astype(o_ref.dtype)


def matmul_acc(a, w, layer, *, tm=1024, tn=1024, tk=2048, out_dtype=F32, name="matmul_acc"):
    m, k = a.shape
    n = w.shape[-1]
    tm = min(tm, m)
    nk = k // tk
    return pl.pallas_call(
        functools.partial(_mm_acc_kernel, nk=nk),
        grid=(m // tm, n // tn, nk),
        in_specs=[pl.BlockSpec((tm, tk), lambda i, j, q: (i, q)),
                  pl.BlockSpec((None, tk, tn), lambda i, j, q: (layer, q, j))],
        out_specs=pl.BlockSpec((tm, tn), lambda i, j, q: (i, j)),
        out_shape=jax.ShapeDtypeStruct((m, n), out_dtype),
        scratch_shapes=[pltpu.VMEM((tm, tn), F32)],
        compiler_params=_cparams("parallel", "parallel", "arbitrary"),
        name=name,
    )(a, w)


def _merge_kernel(yg_ref, ym_ref, wg_ref, wm_ref, gg_ref, gm_ref, o_ref):
    a = jnp.dot(yg_ref[...], wg_ref[...].astype(BF16), preferred_element_type=F32)
    b = jnp.dot(ym_ref[...], wm_ref[...].astype(BF16), preferred_element_type=F32)
    o_ref[...] = (gg_ref[...].astype(F32) * a + gm_ref[...].astype(F32) * b).astype(o_ref.dtype)


def merge_branches(yg, ym, w_o_gdn, w_o_mla, gates, layer, *, tm=1024, tn=512):
    m, k = yg.shape
    d = w_o_gdn.shape[-1]
    tm = min(tm, m)
    nb = d // tn
    a_spec = pl.BlockSpec((tm, k), lambda i, j: (i, 0))
    w_spec = pl.BlockSpec((None, k, tn), lambda i, j: (layer, 0, j))
    return pl.pallas_call(
        _merge_kernel,
        grid=(m // tm, nb),
        in_specs=[a_spec, a_spec, w_spec, w_spec,
                  pl.BlockSpec((tm, tn), lambda i, j: (i, j)),
                  pl.BlockSpec((tm, tn), lambda i, j: (i, nb + j))],
        out_specs=pl.BlockSpec((tm, tn), lambda i, j: (i, j)),
        out_shape=jax.ShapeDtypeStruct((m, d), BF16),
        compiler_params=_cparams("parallel", "parallel"),
        name="merge_branches",
    )(yg, ym, w_o_gdn, w_o_mla, gates, gates)


def _small_kernel(h_ref, w_ref, qn_ref, kvn_ref, cos_ref, sin_ref, alog_ref, dtb_ref,
                  cqn_ref, lat_ref, latb_ref, kr_ref, kpad_ref, gb_ref):
    s = jnp.dot(h_ref[...], w_ref[...], preferred_element_type=F32)
    cqn_ref[...] = _rms(s[:, SM_CQ:SM_CKV], qn_ref[...]).astype(cqn_ref.dtype)
    lat = _rms(s[:, SM_CKV:SM_KR], kvn_ref[...])
    lat_ref[...] = lat
    latb_ref[...] = lat.astype(latb_ref.dtype)
    kp = s[:, SM_KR:SM_KRS] * cos_ref[...] + s[:, SM_KRS:SM_AB] * sin_ref[...]
    kr_ref[...] = kp[:, :MLA_ROPE]
    kpad_ref[...] = kp.astype(kpad_ref.dtype)
    ab = s[:, SM_AB:SM_W]
    x = ab + dtb_ref[...]
    softplus = jnp.maximum(x, 0.0) + jnp.log(1.0 + jnp.exp(-jnp.abs(x)))
    g = -jnp.exp(alog_ref[...]) * softplus
    lane = lax.broadcasted_iota(jnp.int32, ab.shape, 1)
    gb_ref[...] = jnp.where(lane < GDN_HEADS, g, _sigmoid(ab))


def small_proj(h, w_small, layer, qn, kvn, cos_t, sin_t, alog, dtb, tm=512):
    m, k = h.shape
    tm = min(tm, m)
    nt = cos_t.shape[0] // tm
    row = lambda n: pl.BlockSpec((tm, n), lambda i: (i, 0))
    vec = lambda n: pl.BlockSpec((1, n), lambda i: (0, 0))
    tab = pl.BlockSpec((tm, LANE), lambda i: (i % nt, 0))
    return pl.pallas_call(
        _small_kernel,
        grid=(m // tm,),
        in_specs=[row(k), pl.BlockSpec((None, k, SM_W), lambda i: (layer, 0, 0)),
                  vec(MLA_Q_LORA), vec(MLA_KV_LORA), tab, tab, vec(LANE), vec(LANE)],
        out_specs=(row(MLA_Q_LORA), row(MLA_KV_LORA), row(MLA_KV_LORA), row(MLA_ROPE), row(LANE), row(LANE)),
        out_shape=(jax.ShapeDtypeStruct((m, MLA_Q_LORA), BF16),
                   jax.ShapeDtypeStruct((m, MLA_KV_LORA), F32),
                   jax.ShapeDtypeStruct((m, MLA_KV_LORA), BF16),
                   jax.ShapeDtypeStruct((m, MLA_ROPE), F32),
                   jax.ShapeDtypeStruct((m, LANE), BF16),
                   jax.ShapeDtypeStruct((m, LANE), F32)),
        compiler_params=_cparams("parallel"),
        name="small_proj",
    )(h, w_small, qn, kvn, cos_t, sin_t, alog, dtb)


def _mlaq_kernel(cqn_ref, wq_ref, wk_ref, cos_ref, sin_ref, ql_ref, qr_ref):
    qa = jnp.dot(cqn_ref[...], wq_ref[...], preferred_element_type=F32)
    cos = cos_ref[...]
    sin = sin_ref[...]
    nope_w = MLA_HEADS * MLA_NOPE
    for h in range(MLA_HEADS):
        lo = h * LANE
        ql = jnp.dot(qa[:, lo:lo + LANE].astype(BF16), wk_ref[h], preferred_element_type=F32)
        ql_ref[h] = (ql * MLA_SCALE).astype(ql_ref.dtype)
        r = qa[:, nope_w + lo:nope_w + lo + LANE] * cos + qa[:, 2 * nope_w + lo:2 * nope_w + lo + LANE] * sin
        qr_ref[h] = (r * MLA_SCALE).astype(qr_ref.dtype)


def mla_query(cqn, wq_all, wk_t, layer, cos_t, sin_t, out_dtype, tm=256):
    m, k = cqn.shape
    tm = min(tm, m)
    nt = cos_t.shape[0] // tm
    nq = wq_all.shape[-1]
    tab = pl.BlockSpec((tm, LANE), lambda i: (i % nt, 0))
    return pl.pallas_call(
        _mlaq_kernel,
        grid=(m // tm,),
        in_specs=[pl.BlockSpec((tm, k), lambda i: (i, 0)),
                  pl.BlockSpec((None, k, nq), lambda i: (layer, 0, 0)),
                  pl.BlockSpec((None, MLA_HEADS, MLA_NOPE, MLA_KV_LORA), lambda i: (layer, 0, 0, 0)),
                  tab, tab],
        out_specs=(pl.BlockSpec((MLA_HEADS, tm, MLA_KV_LORA), lambda i: (0, i, 0)),
                   pl.BlockSpec((MLA_HEADS, tm, LANE), lambda i: (0, i, 0))),
        out_shape=(jax.ShapeDtypeStruct((MLA_HEADS, m, MLA_KV_LORA), out_dtype),
                   jax.ShapeDtypeStruct((MLA_HEADS, m, LANE), out_dtype)),
        compiler_params=_cparams("parallel"),
        name="mla_query",
    )(cqn, wq_all, wk_t, cos_t, sin_t)


def _uv_kernel(o_ref, w_ref, y_ref):
    for h in range(MLA_HEADS):
        y = jnp.dot(o_ref[h].astype(BF16), w_ref[h], preferred_element_type=F32)
        y_ref[:, h * MLA_V:(h + 1) * MLA_V] = y.astype(y_ref.dtype)


def mla_value_up(o_lat, wuv, layer, tm=512):
    _, m, r = o_lat.shape
    tm = min(tm, m)
    return pl.pallas_call(
        _uv_kernel,
        grid=(m // tm,),
        in_specs=[pl.BlockSpec((MLA_HEADS, tm, r), lambda i: (0, i, 0)),
                  pl.BlockSpec((None, MLA_HEADS, r, MLA_V), lambda i: (layer, 0, 0, 0))],
        out_specs=pl.BlockSpec((tm, MLA_HEADS * MLA_V), lambda i: (i, 0)),
        out_shape=jax.ShapeDtypeStruct((m, MLA_HEADS * MLA_V), BF16),
        compiler_params=_cparams("parallel"),
        name="mla_value_up",
    )(o_lat, wuv)


NEG = -1e30


def _attn_prompt_kernel(ql_ref, qr_ref, k_ref, kr_ref, o_ref, m_scr, l_scr, acc_scr, *, tq, tk, nk):
    i = pl.program_id(1)
    kb = pl.program_id(2)
    rows = MLA_HEADS * tq
    last = (i * tq + tq - 1) // tk

    @pl.when(kb == 0)
    def _():
        m_scr[...] = jnp.full_like(m_scr, NEG)
        l_scr[...] = jnp.zeros_like(l_scr)
        acc_scr[...] = jnp.zeros_like(acc_scr)

    def step(masked):
        q = ql_ref[...].reshape(rows, MLA_KV_LORA)
        qr = qr_ref[...].reshape(rows, LANE)
        k = k_ref[...]
        s = (lax.dot_general(q, k, _NT, preferred_element_type=F32)
             + lax.dot_general(qr, kr_ref[...], _NT, preferred_element_type=F32))
        if masked:
            qpos = i * tq + (lax.broadcasted_iota(jnp.int32, s.shape, 0) & (tq - 1))
            kpos = kb * tk + lax.broadcasted_iota(jnp.int32, s.shape, 1)
            s = jnp.where(kpos <= qpos, s, NEG)
        m_old = m_scr[...]
        m_new = jnp.maximum(m_old, jnp.max(s, axis=-1, keepdims=True))
        p = jnp.exp(s - m_new)
        alpha = jnp.exp(m_old - m_new)
        l_scr[...] = alpha * l_scr[...] + jnp.sum(p, axis=-1, keepdims=True)
        acc_scr[...] = alpha * acc_scr[...] + jnp.dot(p.astype(BF16), k, preferred_element_type=F32)
        m_scr[...] = m_new

    @pl.when(kb < last)
    def _():
        step(False)

    @pl.when(kb == last)
    def _():
        step(True)

    @pl.when(kb == nk - 1)
    def _():
        o = acc_scr[...] / l_scr[...]
        o_ref[...] = o.reshape(MLA_HEADS, tq, MLA_KV_LORA).astype(o_ref.dtype)


def attn_prompt(q_lat, q_rope, latb, kpad, bsz, t, tq=128, tk=1024):
    nq, nk = t // tq, t // tk
    rows = MLA_HEADS * tq
    q4 = q_lat.reshape(MLA_HEADS, bsz, t, MLA_KV_LORA)
    r4 = q_rope.reshape(MLA_HEADS, bsz, t, LANE)
    k3 = latb.reshape(bsz, t, MLA_KV_LORA)
    kr3 = kpad.reshape(bsz, t, LANE)
    kmap = lambda b, i, kb: (b, jnp.minimum(kb, (i * tq + tq - 1) // tk), 0)
    out = pl.pallas_call(
        functools.partial(_attn_prompt_kernel, tq=tq, tk=tk, nk=nk),
        grid=(bsz, nq, nk),
        in_specs=[pl.BlockSpec((MLA_HEADS, None, tq, MLA_KV_LORA), lambda b, i, kb: (0, b, i, 0)),
                  pl.BlockSpec((MLA_HEADS, None, tq, LANE), lambda b, i, kb: (0, b, i, 0)),
                  pl.BlockSpec((None, tk, MLA_KV_LORA), kmap),
                  pl.BlockSpec((None, tk, LANE), kmap)],
        out_specs=pl.BlockSpec((MLA_HEADS, None, tq, MLA_KV_LORA), lambda b, i, kb: (0, b, i, 0)),
        out_shape=jax.ShapeDtypeStruct((MLA_HEADS, bsz, t, MLA_KV_LORA), BF16),
        scratch_shapes=[pltpu.VMEM((rows, 1), F32), pltpu.VMEM((rows, 1), F32),
                        pltpu.VMEM((rows, MLA_KV_LORA), F32)],
        compiler_params=_cparams("parallel", "parallel", "arbitrary"),
        name="attn_prompt",
    )(q4, r4, k3, kr3)
    return out.reshape(MLA_HEADS, bsz * t, MLA_KV_LORA)


def _attn_sample_kernel(pt_ref, ql_ref, qr_ref, *rest, pg, t, ng, nreq):
    np_ = nreq * pg
    lat_refs = rest[:np_]
    krt_refs = rest[np_:2 * np_]
    ln_ref, kn_ref, o_ref, m_scr, l_scr, acc_scr = rest[2 * np_:]
    j = pl.program_id(1)
    rows = MLA_HEADS * t
    reqs = range(nreq)

    @pl.when(j == 0)
    def _():
        m_scr[...] = jnp.full_like(m_scr, NEG)
        l_scr[...] = jnp.zeros_like(l_scr)
        acc_scr[...] = jnp.zeros_like(acc_scr)

    q = [ql_ref[:, a * t:(a + 1) * t, :].reshape(rows, MLA_KV_LORA).astype(BF16) for a in reqs]
    qr = [qr_ref[:, a * t:(a + 1) * t, :].reshape(rows, LANE)[:, :MLA_ROPE].astype(BF16) for a in reqs]

    def update(s, v):
        m_old = [m_scr[a] for a in reqs]
        m_new = [jnp.maximum(m_old[a], jnp.max(s[a], axis=-1, keepdims=True)) for a in reqs]
        p = [jnp.exp(s[a] - m_new[a]) for a in reqs]
        alpha = [jnp.exp(m_old[a] - m_new[a]) for a in reqs]
        pv = [jnp.dot(p[a].astype(BF16), v[a], preferred_element_type=F32) for a in reqs]
        for a in reqs:
            l_scr[a] = alpha[a] * l_scr[a] + jnp.sum(p[a], axis=-1, keepdims=True)
            acc_scr[a] = alpha[a] * acc_scr[a] + pv[a]
            m_scr[a] = m_new[a]

    k = [jnp.concatenate([r[...].astype(BF16) for r in lat_refs[a * pg:(a + 1) * pg]], axis=0) for a in reqs]
    krt = [jnp.concatenate([r[...].astype(BF16) for r in krt_refs[a * pg:(a + 1) * pg]], axis=1) for a in reqs]
    s = [lax.dot_general(q[a], k[a], _NT, preferred_element_type=F32)
         + jnp.dot(qr[a], krt[a], preferred_element_type=F32) for a in reqs]
    update(s, k)

    @pl.when(j == ng - 1)
    def _():
        pad = 2 * SUBLANE - t
        kn = [jnp.concatenate([ln_ref[a * t:(a + 1) * t, :], jnp.zeros((pad, MLA_KV_LORA), F32)],
                              axis=0).astype(BF16) for a in reqs]
        krn = [jnp.concatenate([kn_ref[a * t:(a + 1) * t, :], jnp.zeros((pad, MLA_ROPE), F32)],
                               axis=0).astype(BF16) for a in reqs]
        sn = [lax.dot_general(q[a], kn[a], _NT, preferred_element_type=F32)
              + lax.dot_general(qr[a], krn[a], _NT, preferred_element_type=F32) for a in reqs]
        qt = lax.broadcasted_iota(jnp.int32, sn[0].shape, 0) & (t - 1)
        kt = lax.broadcasted_iota(jnp.int32, sn[0].shape, 1)
        update([jnp.where(kt <= qt, sn[a], NEG) for a in reqs], kn)
        for a in reqs:
            o = acc_scr[a] / l_scr[a]
            o_ref[:, a * t:(a + 1) * t, :] = o.reshape(MLA_HEADS, t, MLA_KV_LORA).astype(o_ref.dtype)


def attn_sample(q_lat, q_rope, cache_lat, cache_krt, page_table, lat_new, kr_new, layer, db, t, pg=8, nreq=4):
    n_pages = page_table.shape[1]
    ng = n_pages // pg
    pt = page_table.reshape(-1)
    rows = MLA_HEADS * t

    def page_spec(shape, a, p):
        return pl.BlockSpec((None, None) + shape,
                            lambda b, j, pt_ref: (layer, pt_ref[(b * nreq + a) * n_pages + j * pg + p], 0, 0))

    pages = [(a, p) for a in range(nreq) for p in range(pg)]
    grid_spec = pltpu.PrefetchScalarGridSpec(
        num_scalar_prefetch=1,
        grid=(db // nreq, ng),
        in_specs=([pl.BlockSpec((MLA_HEADS, nreq * t, MLA_KV_LORA), lambda b, j, pt_ref: (0, b, 0)),
                   pl.BlockSpec((MLA_HEADS, nreq * t, LANE), lambda b, j, pt_ref: (0, b, 0))]
                  + [page_spec((PAGE_SIZE, MLA_KV_LORA), a, p) for a, p in pages]
                  + [page_spec((MLA_ROPE, PAGE_SIZE), a, p) for a, p in pages]
                  + [pl.BlockSpec((nreq * t, MLA_KV_LORA), lambda b, j, pt_ref: (b, 0)),
                     pl.BlockSpec((nreq * t, MLA_ROPE), lambda b, j, pt_ref: (b, 0))]),
        out_specs=pl.BlockSpec((MLA_HEADS, nreq * t, MLA_KV_LORA), lambda b, j, pt_ref: (0, b, 0)),
        scratch_shapes=[pltpu.VMEM((nreq, rows, 1), F32), pltpu.VMEM((nreq, rows, 1), F32),
                        pltpu.VMEM((nreq, rows, MLA_KV_LORA), F32)],
    )
    n = nreq * pg
    return pl.pallas_call(
        functools.partial(_attn_sample_kernel, pg=pg, t=t, ng=ng, nreq=nreq),
        grid_spec=grid_spec,
        out_shape=jax.ShapeDtypeStruct((MLA_HEADS, db * t, MLA_KV_LORA), F32),
        compiler_params=_cparams("parallel", "arbitrary"),
        name="attn_sample",
    )(pt, q_lat, q_rope, *([cache_lat] * n), *([cache_krt] * n), lat_new, kr_new)


def _cumsum_rows(x, seg):
    row = lax.broadcasted_iota(jnp.int32, x.shape, 0) & (seg - 1)
    s = 1
    while s < seg:
        x = x + jnp.where(row >= s, pltpu.roll(x, s, 0), 0.0)
        s *= 2
    return x


def _col(x, lane):
    li = lax.broadcasted_iota(jnp.int32, x.shape, 1)
    return jnp.sum(jnp.where(li == lane, x, 0.0), axis=1, keepdims=True)


def _rowvec(xt, sub):
    si = lax.broadcasted_iota(jnp.int32, xt.shape, 0)
    return jnp.sum(jnp.where(si == sub, xt, 0.0), axis=0, keepdims=True)


def _transpose_block(x):
    c = x.shape[0]
    if c < LANE:
        x = jnp.concatenate([x, jnp.zeros((LANE - c, LANE), x.dtype)], axis=0)
    return x.T[:, :c]


def _l2norm_heads(x, nh):
    outs = []
    for j in range(nh):
        xh = x[:, j * LANE:(j + 1) * LANE]
        outs.append(xh * lax.rsqrt(jnp.sum(xh * xh, axis=-1, keepdims=True) + EPS))
    return outs


def _chunk_masks(c, same, nlevel):
    ri = lax.broadcasted_iota(jnp.int32, (c, c), 0)
    ci = lax.broadcasted_iota(jnp.int32, (c, c), 1)
    tri = ri >= ci
    strict = ri > ci
    if same is not None:
        tri = jnp.logical_and(tri, same)
        strict = jnp.logical_and(strict, same)
    lower = [jnp.logical_and((ri >> k) == (ci >> k) + 1, (ri >> (k + 1)) == (ci >> (k + 1)))
             for k in range(nlevel)]
    return tri, strict, ri == ci, lower


def _chunk_prep(q, k, v, gcol, grow, bcol, glast, masks):
    tri, strict, diag, lower = masks
    n = range(len(q))
    qs = [q[i] * (GDN_DK ** -0.5) for i in n]
    kb = [k[i].astype(BF16) for i in n]
    kk = [lax.dot_general(kb[i], kb[i], _NT, preferred_element_type=F32) for i in n]
    qk = [lax.dot_general(qs[i].astype(BF16), kb[i], _NT, preferred_element_type=F32) for i in n]
    decay = [jnp.where(tri, jnp.exp(jnp.where(tri, gcol[i] - grow[i], 0.0)), 0.0) for i in n]
    lmat = [jnp.where(strict, bcol[i] * kk[i] * decay[i], 0.0) for i in n]
    eye = jnp.where(diag, 1.0, 0.0)
    x = [eye - jnp.where(lower[0], lmat[i], 0.0) for i in n]
    for m in lower[1:]:
        xb = [x[i].astype(BF16) for i in n]
        t = [jnp.dot(xb[i], jnp.where(m, lmat[i], 0.0).astype(BF16), preferred_element_type=F32) for i in n]
        x = [x[i] - jnp.dot(t[i].astype(BF16), xb[i], preferred_element_type=F32) for i in n]
    eg = [jnp.exp(gcol[i]) for i in n]
    uw = [_bdot(x[i], jnp.concatenate([v[i] * bcol[i], k[i] * (bcol[i] * eg[i])], axis=1)) for i in n]
    u = [uw[i][:, :GDN_DV] for i in n]
    w = [uw[i][:, GDN_DV:] for i in n]
    qkd = [qk[i] * decay[i] for i in n]
    qd = [qs[i] * eg[i] for i in n]
    kd = [k[i] * jnp.exp(glast[i] - gcol[i]) for i in n]
    return u, w, qkd, qd, kd


def _gated_out(o, z, nw):
    y = o * lax.rsqrt(jnp.mean(o * o, axis=-1, keepdims=True) + EPS) * nw
    return y * (z * _sigmoid(z))


def _gdn_prompt_kernel(q_ref, k_ref, v_ref, z_ref, hq_ref, hk_ref, hv_ref, cwq_ref, cwk_ref, cwv_ref,
                       gb_ref, nw_ref, y_ref, so_ref, s_scr, *, hb, rows, nr):
    hg = pl.program_id(1)
    r = pl.program_id(2)
    c = GDN_CHUNK

    @pl.when(r == 0)
    def _():
        s_scr[...] = jnp.zeros_like(s_scr)

    def conv(x_ref, h_ref, w_ref):
        u = x_ref[...]
        halo = jnp.where(r > 0, h_ref[...], 0.0)
        w = w_ref[...]
        acc = u * w[GDN_CONV - 1:GDN_CONV, :]
        row8 = lax.broadcasted_iota(jnp.int32, halo.shape, 0)
        for s in range(1, GDN_CONV):
            ru = pltpu.roll(u, s, 0)
            top = jnp.where(row8 < s, pltpu.roll(halo, s, 0), ru[:SUBLANE])
            sh = jnp.concatenate([top, ru[SUBLANE:]], axis=0)
            acc = acc + sh * w[GDN_CONV - 1 - s:GDN_CONV - s, :]
        return acc * _sigmoid(acc)

    qh = _l2norm_heads(conv(q_ref, hq_ref, cwq_ref), hb)
    kh = _l2norm_heads(conv(k_ref, hk_ref, cwk_ref), hb)
    vv = conv(v_ref, hv_ref, cwv_ref)
    gb = gb_ref[...]
    gc_all = _cumsum_rows(gb, c)
    nw = nw_ref[...]
    masks = _chunk_masks(c, None, c.bit_length() - 1)
    nc = rows // c
    qs_, ks_, vs_, gcol, grow, bcol, glast = [], [], [], [], [], [], []
    for ch in range(nc):
        sl = slice(ch * c, (ch + 1) * c)
        gc = gc_all[sl]
        gct = _transpose_block(gc)
        for j in range(hb):
            head = hg * hb + j
            qs_.append(qh[j][sl])
            ks_.append(kh[j][sl])
            vs_.append(vv[sl, j * LANE:(j + 1) * LANE])
            gcol.append(_col(gc, head))
            bcol.append(_col(gb[sl], GDN_HEADS + head))
            grow.append(_rowvec(gct, head))
            glast.append(gcol[-1][c - 1:c, :])
    u, w, qkd, qd, kd = _chunk_prep(qs_, ks_, vs_, gcol, grow, bcol, glast, masks)
    heads = range(hb)
    state = [s_scr[j] for j in heads]
    for ch in range(nc):
        sl = slice(ch * c, (ch + 1) * c)
        g = [ch * hb + j for j in heads]
        wq = [_bdot(jnp.concatenate([w[g[j]], qd[g[j]]], axis=0), state[j]) for j in heads]
        v_new = [u[g[j]] - wq[j][:c] for j in heads]
        o = [wq[j][c:] + _bdot(qkd[g[j]], v_new[j]) for j in heads]
        state = [state[j] * jnp.exp(glast[g[j]]) + lax.dot_general(
            kd[g[j]].astype(BF16), v_new[j].astype(BF16), _TN, preferred_element_type=F32) for j in heads]
        for j in heads:
            y_ref[sl, j * LANE:(j + 1) * LANE] = _gated_out(
                o[j], z_ref[sl, j * LANE:(j + 1) * LANE], nw).astype(y_ref.dtype)
    for j in heads:
        s_scr[j] = state[j]

    @pl.when(r == nr - 1)
    def _():
        so_ref[...] = s_scr[...]


def gdn_prompt(qkvz, gb, conv_w, norm_w, layer, bsz, t, hb=4, rows=512):
    w = hb * LANE
    nr = t // rows
    x3 = qkvz.reshape(bsz, t, -1)
    gb3 = gb.reshape(bsz, t, LANE)
    kq, kv_, kz = GDN_QK_W // w, 2 * GDN_QK_W // w, CONV_DIM // w
    hr = rows // SUBLANE

    def xs(off):
        return pl.BlockSpec((None, rows, w), lambda b, g, r: (b, r, off + g))

    def hs(off):
        return pl.BlockSpec((None, SUBLANE, w), lambda b, g, r: (b, jnp.maximum(r * hr - 1, 0), off + g))

    def ws(off):
        return pl.BlockSpec((None, GDN_CONV, w), lambda b, g, r: (layer, 0, off + g))

    return pl.pallas_call(
        functools.partial(_gdn_prompt_kernel, hb=hb, rows=rows, nr=nr),
        grid=(bsz, GDN_HEADS // hb, nr),
        in_specs=[xs(0), xs(kq), xs(kv_), xs(kz), hs(0), hs(kq), hs(kv_), ws(0), ws(kq), ws(kv_),
                  pl.BlockSpec((None, rows, LANE), lambda b, g, r: (b, r, 0)),
                  pl.BlockSpec((1, GDN_DV), lambda b, g, r: (0, 0))],
        out_specs=(pl.BlockSpec((None, rows, w), lambda b, g, r: (b, r, g)),
                   pl.BlockSpec((None, hb, GDN_DK, GDN_DV), lambda b, g, r: (b, g, 0, 0))),
        out_shape=(jax.ShapeDtypeStruct((bsz, t, GDN_V_W), BF16),
                   jax.ShapeDtypeStruct((bsz, GDN_HEADS, GDN_DK, GDN_DV), F32)),
        scratch_shapes=[pltpu.VMEM((hb, GDN_DK, GDN_DV), F32)],
        compiler_params=_cparams("parallel", "parallel", "arbitrary"),
        name="gdn_prompt",
    )(x3, x3, x3, x3, x3, x3, x3, conv_w, conv_w, conv_w, gb3, norm_w)


def _gdn_sample_kernel(q_ref, k_ref, v_ref, z_ref, hq_ref, hk_ref, hv_ref, cwq_ref, cwk_ref, cwv_ref,
                       gb_ref, nw_ref, s0_ref, y_ref, so_ref, *, hb, t, nb):
    hg = pl.program_id(1)
    c = nb * t

    def conv(x_ref, h_ref, w_ref):
        u = x_ref[...]
        halo = h_ref[...]
        w = w_ref[...]
        acc = u * w[GDN_CONV - 1:GDN_CONV, :]
        tt = lax.broadcasted_iota(jnp.int32, u.shape, 0) & (t - 1)
        for s in range(1, GDN_CONV):
            sh = jnp.where(tt < s, pltpu.roll(halo, c - t + s, 0), pltpu.roll(u, s, 0))
            acc = acc + sh * w[GDN_CONV - 1 - s:GDN_CONV - s, :]
        return acc * _sigmoid(acc)

    qh = _l2norm_heads(conv(q_ref, hq_ref, cwq_ref), hb)
    kh = _l2norm_heads(conv(k_ref, hk_ref, cwk_ref), hb)
    vv = conv(v_ref, hv_ref, cwv_ref)
    gb = gb_ref[...]
    gc = _cumsum_rows(gb, t)
    gl = jnp.broadcast_to(gc.reshape(nb, t, LANE)[:, t - 1:t, :], (nb, t, LANE)).reshape(c, LANE)
    gct = _transpose_block(gc)
    nw = nw_ref[...]
    shift = t.bit_length() - 1
    same = (lax.broadcasted_iota(jnp.int32, (c, c), 0) >> shift) == (lax.broadcasted_iota(jnp.int32, (c, c), 1) >> shift)
    masks = _chunk_masks(c, same, shift)
    rowreq = lax.broadcasted_iota(jnp.int32, (c, LANE), 0) >> shift
    heads = range(hb)
    reqs = range(nb)
    gcol = [_col(gc, hg * hb + j) for j in heads]
    bcol = [_col(gb, GDN_HEADS + hg * hb + j) for j in heads]
    grow = [_rowvec(gct, hg * hb + j) for j in heads]
    glast = [_col(gl, hg * hb + j) for j in heads]
    u, w, qkd, qd, kd = _chunk_prep(qh, kh, [vv[:, j * LANE:(j + 1) * LANE] for j in heads],
                                    gcol, grow, bcol, glast, masks)
    wq = [[_bdot(jnp.concatenate([w[j][b * t:(b + 1) * t], qd[j][b * t:(b + 1) * t]], axis=0), s0_ref[b, j])
           for b in reqs] for j in heads]
    v_new = [jnp.concatenate([u[j][b * t:(b + 1) * t] - wq[j][b][:t] for b in reqs], axis=0) for j in heads]
    o = [jnp.concatenate([wq[j][b][t:] for b in reqs], axis=0) + _bdot(qkd[j], v_new[j]) for j in heads]
    for j in heads:
        vb = v_new[j].astype(BF16)
        for b in reqs:
            kdb = jnp.where(rowreq == b, kd[j], 0.0).astype(BF16)
            so_ref[b, j] = s0_ref[b, j] * jnp.exp(glast[j][b * t:b * t + 1, :]) + lax.dot_general(
                kdb, vb, _TN, preferred_element_type=F32)
        y_ref[:, j * LANE:(j + 1) * LANE] = _gated_out(
            o[j], z_ref[:, j * LANE:(j + 1) * LANE], nw).astype(y_ref.dtype)


def gdn_sample(qkvz, gb, conv_w, norm_w, halo, state, layer, db, t, hb=8, nb=8):
    w = hb * LANE
    c = nb * t
    kq, kv_, kz = GDN_QK_W // w, 2 * GDN_QK_W // w, CONV_DIM // w

    def xs(off):
        return pl.BlockSpec((c, w), lambda i, g: (i, off + g))

    def ws(off):
        return pl.BlockSpec((None, GDN_CONV, w), lambda i, g: (layer, 0, off + g))

    return pl.pallas_call(
        functools.partial(_gdn_sample_kernel, hb=hb, t=t, nb=nb),
        grid=(db // nb, GDN_HEADS // hb),
        in_specs=[xs(0), xs(kq), xs(kv_), xs(kz), xs(0), xs(kq), xs(kv_), ws(0), ws(kq), ws(kv_),
                  pl.BlockSpec((c, LANE), lambda i, g: (i, 0)),
                  pl.BlockSpec((1, GDN_DV), lambda i, g: (0, 0)),
                  pl.BlockSpec((None, nb, hb, GDN_DK, GDN_DV), lambda i, g: (layer, i, g, 0, 0))],
        out_specs=(pl.BlockSpec((c, w), lambda i, g: (i, g)),
                   pl.BlockSpec((nb, hb, GDN_DK, GDN_DV), lambda i, g: (i, g, 0, 0))),
        out_shape=(jax.ShapeDtypeStruct((db * t, GDN_V_W), BF16),
                   jax.ShapeDtypeStruct((db, GDN_HEADS, GDN_DK, GDN_DV), F32)),
        compiler_params=_cparams("parallel", "parallel"),
        name="gdn_sample",
    )(qkvz, qkvz, qkvz, qkvz, halo, halo, halo, conv_w, conv_w, conv_w, gb, norm_w, state)


def _swap_halves(x):
    half = x.shape[-1] // 2
    return jnp.concatenate([x[..., half:], x[..., :half]], axis=-1)


def _pack_small(w_in):
    depth, d, _ = w_in.shape
    o_a = CONV_DIM + GDN_V_W
    o_b = o_a + GDN_HEADS
    o_cq = o_b + GDN_HEADS
    o_ckv = o_cq + MLA_Q_LORA
    o_kr = o_ckv + MLA_KV_LORA
    o_g = o_kr + MLA_ROPE
    z = lambda n: jnp.zeros((depth, d, n), w_in.dtype)
    kr = w_in[:, :, o_kr:o_g]
    small = jnp.concatenate([
        w_in[:, :, o_cq:o_ckv], w_in[:, :, o_ckv:o_kr],
        kr, z(LANE - MLA_ROPE), _swap_halves(kr), z(LANE - MLA_ROPE),
        w_in[:, :, o_a:o_b], w_in[:, :, o_b:o_cq], z(LANE - 2 * GDN_HEADS)], axis=-1).astype(BF16)
    gates = w_in[:, :, o_g:].astype(BF16)
    qkvz = w_in[:, :, :o_a].astype(BF16)
    return small, gates, qkvz


def _pack_mla(w_q_b, w_kv_b):
    depth = w_q_b.shape[0]
    wq = w_q_b.reshape(depth, MLA_Q_LORA, MLA_HEADS, MLA_NOPE + MLA_ROPE)
    nope = wq[..., :MLA_NOPE].reshape(depth, MLA_Q_LORA, -1)
    rope = wq[..., MLA_NOPE:]
    pad = jnp.zeros(rope.shape[:-1] + (LANE - MLA_ROPE,), rope.dtype)
    rope_p = jnp.concatenate([rope, pad], axis=-1).reshape(depth, MLA_Q_LORA, -1)
    rope_s = jnp.concatenate([_swap_halves(rope), pad], axis=-1).reshape(depth, MLA_Q_LORA, -1)
    wq_all = jnp.concatenate([nope, rope_p, rope_s], axis=-1).astype(BF16)
    wkv = w_kv_b.reshape(depth, MLA_KV_LORA, MLA_HEADS, MLA_NOPE + MLA_V)
    wk_t = wkv[..., :MLA_NOPE].transpose(0, 2, 3, 1).astype(BF16)
    wuv = wkv[..., MLA_NOPE:].transpose(0, 2, 1, 3).astype(BF16)
    return wq_all, wk_t, wuv


def _rope_tables(pos):
    half = MLA_ROPE // 2
    inv_freq = ROPE_THETA ** (-jnp.arange(half, dtype=F32) / half)
    ang = pos.astype(F32)[:, None] * inv_freq[None, :]
    cos, sin = jnp.cos(ang), jnp.sin(ang)
    z = jnp.zeros((pos.shape[0], LANE - MLA_ROPE), F32)
    return jnp.concatenate([cos, cos, z], axis=-1), jnp.concatenate([-sin, sin, z], axis=-1)


def _lane_pad(v, n=LANE):
    return jnp.concatenate([v, jnp.zeros((n - v.shape[0],), v.dtype)]).reshape(1, n)


def _layer(x, h, mods, l, wts, tabs, group):
    bsz, t, d = x.shape
    m = bsz * t
    sh_a, sc_a, gt_a, sh_m, sc_m, gt_m = mods[l]
    h2 = h.reshape(m, d)
    qkvz = matmul(h2, wts["w_qkvz"], l, CONV_DIM + GDN_V_W, tn=1024, name="in_proj_gdn")
    gates = matmul(h2, wts["w_gates"], l, 2 * d, tn=1024, out_dtype=BF16, act="sigmoid", name="in_proj_gates")
    cos_t, sin_t = tabs
    cqn, lat, latb, kr, kpad, gb = small_proj(
        h2, wts["w_small"], l, wts["q_a_norm"][l:l + 1], wts["kv_a_norm"][l:l + 1], cos_t, sin_t,
        wts["alog"][l], wts["dtb"][l])
    norm_w = wts["gdn_norm"][l:l + 1]
    if group["kind"] == "prompt":
        y_gdn, s_new = gdn_prompt(qkvz, gb, wts["conv_w"], norm_w, l, bsz, t)
        y_gdn = y_gdn.reshape(m, GDN_V_W)
        q_lat, q_rope = mla_query(cqn, wts["wq_all"], wts["wk_t"], l, cos_t, sin_t, BF16)
        o_lat = attn_prompt(q_lat, q_rope, latb, kpad, bsz, t)
    else:
        y_gdn, s_new = gdn_sample(qkvz, gb, wts["conv_w"], norm_w, group["halo"][l], group["state"], l, bsz, t)
        q_lat, q_rope = mla_query(cqn, wts["wq_all"], wts["wk_t"], l, cos_t, sin_t, F32)
        o_lat = attn_sample(q_lat, q_rope, group["cache_lat"], group["cache_kr"], group["page_table"],
                            lat, kr, l, bsz, t)
    y_mla = mla_value_up(o_lat, wts["wuv"], l)
    merged = merge_branches(y_gdn, y_mla, wts["w_o_gdn"], wts["w_o_mla"], gates, l)
    mix = matmul(merged, wts["w_out"], l, d, name="out_proj").reshape(bsz, t, d)
    g4 = wts["norm_g"][l]
    x, hm = resid_norm(x, mix, g4, 1, gt_a, pre=(g4, 2, sc_m, sh_m))
    f = matmul(hm.reshape(m, d), wts["w_up"], l, wts["w_up"].shape[-1], out_dtype=BF16, act="relu2", name="ffn_up")
    f = matmul_acc(f, wts["w_down"], l, name="ffn_down").reshape(bsz, t, d)
    if l + 1 < len(mods):
        nsh, nsc = mods[l + 1][0], mods[l + 1][1]
        x, h_next = resid_norm(x, f, g4, 3, gt_m, pre=(wts["norm_g"][l + 1], 0, nsc, nsh))
    else:
        x, h_next = resid_norm(x, f, g4, 3, gt_m), None
    conv_new = qkvz.reshape(bsz, t, -1)[:, t - (GDN_CONV - 1):, :CONV_DIM]
    return x, h_next, (lat.reshape(bsz, t, -1), kr.reshape(bsz, t, -1), s_new, conv_new)


def kernel(x_prompt, x_sample, cache_mla_latent, cache_mla_krope, state_gdn, state_conv, page_table, c_prompt, c_sample, w_in, conv_w, a_log, dt_bias, gdn_norm, q_a_norm, w_q_b, kv_a_norm, w_kv_b, w_o_gdn, w_o_mla, w_out, w_up, w_down, w_ada, b_ada, norm_g):
    bsz, seq, d = x_prompt.shape
    db, t = x_sample.shape[0], x_sample.shape[1]
    depth = w_in.shape[0]
    past_len = page_table.shape[1] * PAGE_SIZE

    w_small, w_gates, w_qkvz = _pack_small(w_in)
    wq_all, wk_t, wuv = _pack_mla(w_q_b, w_kv_b)
    wts = dict(w_qkvz=w_qkvz, w_gates=w_gates, w_small=w_small, q_a_norm=q_a_norm, kv_a_norm=kv_a_norm,
               alog=jnp.stack([_lane_pad(a_log[l]) for l in range(depth)]),
               dtb=jnp.stack([_lane_pad(dt_bias[l]) for l in range(depth)]),
               gdn_norm=gdn_norm, conv_w=conv_w, wq_all=wq_all, wk_t=wk_t, wuv=wuv,
               w_o_gdn=w_o_gdn, w_o_mla=w_o_mla, w_out=w_out, w_up=w_up, w_down=w_down, norm_g=norm_g)

    tabs_p = _rope_tables(jnp.arange(seq, dtype=jnp.int32))
    rows_s = min(512, db * t)
    tabs_s = _rope_tables(past_len + (jnp.arange(rows_s, dtype=jnp.int32) % t))

    nc = bsz + db
    mc = -(-nc // 16) * 16
    c_all = jnp.concatenate([c_prompt, c_sample, jnp.zeros((mc - nc, d), F32)], axis=0)
    mod = ada_mod(c_all, w_ada, b_ada)

    def split_mods(lo, n):
        return [tuple(mod[l, lo:lo + n, i * d:(i + 1) * d].reshape(n, 1, d) for i in range(N_MOD))
                for l in range(depth)]

    mods_p = split_mods(0, bsz)
    mods_s = split_mods(bsz, db)

    halo = jnp.pad(state_conv, ((0, 0), (0, 0), (t - (GDN_CONV - 1), 0), (0, 0))).reshape(depth, db * t, CONV_DIM)
    grp_p = dict(kind="prompt")
    grp_s = dict(kind="sample", halo=halo, state=state_gdn, cache_lat=cache_mla_latent,
                 cache_kr=jnp.swapaxes(cache_mla_krope, 2, 3), page_table=page_table)

    xp, xs = x_prompt, x_sample
    hp = norm_mod(xp, norm_g[0], 0, mods_p[0][1], mods_p[0][0])
    hs = norm_mod(xs, norm_g[0], 0, mods_s[0][1], mods_s[0][0])
    leaves_p, leaves_s = [], []
    for l in range(depth):
        xp, hp, lp = _layer(xp, hp, mods_p, l, wts, tabs_p, grp_p)
        xs, hs, ls = _layer(xs, hs, mods_s, l, wts, tabs_s, grp_s)
        leaves_p.append(lp)
        leaves_s.append(ls)
    stack = lambda leaves, i: jnp.stack([lv[i] for lv in leaves])
    return (xp, xs, stack(leaves_p, 0), stack(leaves_p, 1), stack(leaves_p, 2), stack(leaves_p, 3),
            stack(leaves_s, 0), stack(leaves_s, 1), stack(leaves_s, 2), stack(leaves_s, 3))
```

```python
import functools

import numpy as np
import jax
import jax.numpy as jnp
from jax import lax
from jax.experimental import pallas as pl
from jax.experimental.pallas import tpu as pltpu

F32 = jnp.float32
BF16 = jnp.bfloat16

GDN_HEADS = 16
GDN_DK = 128
GDN_DV = 128
GDN_CONV = 4
GDN_CHUNK = 64
GDN_QK_W = GDN_HEADS * GDN_DK
GDN_V_W = GDN_HEADS * GDN_DV
CONV_DIM = 2 * GDN_QK_W + GDN_V_W
MLA_HEADS = 16
MLA_Q_LORA = 768
MLA_KV_LORA = 256
MLA_NOPE = 128
MLA_ROPE = 64
MLA_V = 128
MLA_SCALE = (MLA_NOPE + MLA_ROPE) ** -0.5
ROPE_THETA = 10000.0
PAGE_SIZE = 128
N_MOD = 6
EPS = 1e-6

LANE = 128
SUBLANE = 8
VMEM_LIMIT = 56 * 1024 * 1024

SM_CQ = 0
SM_CKV = MLA_Q_LORA
SM_KR = SM_CKV + MLA_KV_LORA
SM_KRS = SM_KR + LANE
SM_AB = SM_KRS + LANE
SM_W = SM_AB + LANE


def _cparams(*sem):
    return pltpu.CompilerParams(dimension_semantics=sem, vmem_limit_bytes=VMEM_LIMIT)


def _sigmoid(x):
    return 1.0 / (1.0 + jnp.exp(-x))


def _bdot(a, b):
    return jnp.dot(a.astype(BF16), b.astype(BF16), preferred_element_type=F32)


_NT = (((1,), (1,)), ((), ()))
_TN = (((0,), (0,)), ((), ()))


def _ada_kernel(c_ref, w_ref, b_ref, o_ref):
    c = c_ref[...]
    a = (c * _sigmoid(c)).astype(BF16)
    o_ref[...] = jnp.dot(a, w_ref[...].astype(BF16), preferred_element_type=F32) + b_ref[...]


def ada_mod(c, w_ada, b_ada, tn=512):
    depth, d, n = w_ada.shape
    mc = c.shape[0]
    return pl.pallas_call(
        _ada_kernel,
        grid=(depth, n // tn),
        in_specs=[pl.BlockSpec((mc, d), lambda l, j: (0, 0)),
                  pl.BlockSpec((None, d, tn), lambda l, j: (l, 0, j)),
                  pl.BlockSpec((None, 1, tn), lambda l, j: (l, 0, j))],
        out_specs=pl.BlockSpec((None, mc, tn), lambda l, j: (l, 0, j)),
        out_shape=jax.ShapeDtypeStruct((depth, mc, n), F32),
        compiler_params=_cparams("parallel", "parallel"),
        name="ada_mod",
    )(c, w_ada, b_ada.reshape(depth, 1, n))


def _rms(x, g):
    return x * lax.rsqrt(jnp.mean(x * x, axis=-1, keepdims=True) + EPS) * g


def _normmod_kernel(x_ref, g_ref, sc_ref, sh_ref, h_ref, *, gi):
    y = _rms(x_ref[...], g_ref[gi:gi + 1, :])
    h_ref[...] = (y * (1.0 + sc_ref[...]) + sh_ref[...]).astype(h_ref.dtype)


def _row_blocks(bsz, t, rows):
    if t >= rows:
        return 1, rows
    return rows // t, t


def norm_mod(x, g4, gi, sc, sh, rows=256):
    bsz, t, d = x.shape
    bb, tt = _row_blocks(bsz, t, rows)
    xs = pl.BlockSpec((bb, tt, d), lambda b, i: (b, i, 0))
    ms = pl.BlockSpec((bb, 1, d), lambda b, i: (b, 0, 0))
    return pl.pallas_call(
        functools.partial(_normmod_kernel, gi=gi),
        grid=(bsz // bb, t // tt),
        in_specs=[xs, pl.BlockSpec(g4.shape, lambda b, i: (0, 0)), ms, ms],
        out_specs=xs,
        out_shape=jax.ShapeDtypeStruct(x.shape, BF16),
        compiler_params=_cparams("parallel", "parallel"),
        name="norm_mod",
    )(x, g4, sc, sh)


def _resid_kernel(x_ref, y_ref, gpost_ref, gt_ref, *rest, gi_post, gi_pre):
    xn = x_ref[...] + gt_ref[...] * _rms(y_ref[...], gpost_ref[gi_post:gi_post + 1, :])
    if gi_pre is None:
        (xo_ref,) = rest
        xo_ref[...] = xn
    else:
        gpre_ref, sc_ref, sh_ref, xo_ref, h_ref = rest
        xo_ref[...] = xn
        h = _rms(xn, gpre_ref[gi_pre:gi_pre + 1, :])
        h_ref[...] = (h * (1.0 + sc_ref[...]) + sh_ref[...]).astype(h_ref.dtype)


def resid_norm(x, y, gpost4, gi_post, gt, pre=None, rows=256):
    bsz, t, d = x.shape
    bb, tt = _row_blocks(bsz, t, rows)
    xs = pl.BlockSpec((bb, tt, d), lambda b, i: (b, i, 0))
    ms = pl.BlockSpec((bb, 1, d), lambda b, i: (b, 0, 0))
    gs = pl.BlockSpec(gpost4.shape, lambda b, i: (0, 0))
    ins = [x, y, gpost4, gt]
    in_specs = [xs, xs, gs, ms]
    if pre is None:
        gi_pre = None
        out_shape = jax.ShapeDtypeStruct(x.shape, F32)
        out_specs = xs
    else:
        gpre4, gi_pre, sc, sh = pre
        ins += [gpre4, sc, sh]
        in_specs += [gs, ms, ms]
        out_shape = (jax.ShapeDtypeStruct(x.shape, F32), jax.ShapeDtypeStruct(x.shape, BF16))
        out_specs = (xs, xs)
    return pl.pallas_call(
        functools.partial(_resid_kernel, gi_post=gi_post, gi_pre=gi_pre),
        grid=(bsz // bb, t // tt),
        in_specs=in_specs, out_specs=out_specs, out_shape=out_shape,
        compiler_params=_cparams("parallel", "parallel"),
        name="resid_norm",
    )(*ins)


def _mm_kernel(a_ref, b_ref, o_ref, *, act):
    acc = jnp.dot(a_ref[...], b_ref[...].astype(BF16), preferred_element_type=F32)
    if act == "relu2":
        r = jnp.maximum(acc, 0.0)
        acc = r * r
    elif act == "sigmoid":
        acc = _sigmoid(acc)
    o_ref[...] = acc.astype(o_ref.dtype)


def matmul(a, w, layer, n, *, col0=0, tm=1024, tn=512, out_dtype=F32, act=None, name="matmul"):
    m, k = a.shape
    tm = min(tm, m)
    cb = col0 // tn
    return pl.pallas_call(
        functools.partial(_mm_kernel, act=act),
        grid=(m // tm, n // tn),
        in_specs=[pl.BlockSpec((tm, k), lambda i, j: (i, 0)),
                  pl.BlockSpec((None, k, tn), lambda i, j: (layer, 0, cb + j))],
        out_specs=pl.BlockSpec((tm, tn), lambda i, j: (i, j)),
        out_shape=jax.ShapeDtypeStruct((m, n), out_dtype),
        compiler_params=_cparams("parallel", "parallel"),
        name=name,
    )(a, w)


def _mm_acc_kernel(a_ref, b_ref, o_ref, acc_ref, *, nk):
    kk = pl.program_id(2)

    @pl.when(kk == 0)
    def _():
        acc_ref[...] = jnp.zeros_like(acc_ref)

    acc_ref[...] += jnp.dot(a_ref[...], b_ref[...].astype(BF16), preferred_element_type=F32)

    @pl.when(kk == nk - 1)
    def _():
        o_ref[...] = acc_ref[...].astype(o_ref.dtype)


def matmul_acc(a, w, layer, *, tm=1024, tn=1024, tk=2048, out_dtype=F32, name="matmul_acc"):
    m, k = a.shape
    n = w.shape[-1]
    tm = min(tm, m)
    nk = k // tk
    return pl.pallas_call(
        functools.partial(_mm_acc_kernel, nk=nk),
        grid=(m // tm, n // tn, nk),
        in_specs=[pl.BlockSpec((tm, tk), lambda i, j, q: (i, q)),
                  pl.BlockSpec((None, tk, tn), lambda i, j, q: (layer, q, j))],
        out_specs=pl.BlockSpec((tm, tn), lambda i, j, q: (i, j)),
        out_shape=jax.ShapeDtypeStruct((m, n), out_dtype),
        scratch_shapes=[pltpu.VMEM((tm, tn), F32)],
        compiler_params=_cparams("parallel", "parallel", "arbitrary"),
        name=name,
    )(a, w)


def _merge_kernel(yg_ref, ym_ref, wg_ref, wm_ref, gg_ref, gm_ref, o_ref):
    a = jnp.dot(yg_ref[...], wg_ref[...].astype(BF16), preferred_element_type=F32)
    b = jnp.dot(ym_ref[...], wm_ref[...].astype(BF16), preferred_element_type=F32)
    o_ref[...] = (gg_ref[...].astype(F32) * a + gm_ref[...].astype(F32) * b).astype(o_ref.dtype)


def merge_branches(yg, ym, w_o_gdn, w_o_mla, gates, layer, *, tm=1024, tn=512):
    m, k = yg.shape
    d = w_o_gdn.shape[-1]
    tm = min(tm, m)
    nb = d // tn
    a_spec = pl.BlockSpec((tm, k), lambda i, j: (i, 0))
    w_spec = pl.BlockSpec((None, k, tn), lambda i, j: (layer, 0, j))
    return pl.pallas_call(
        _merge_kernel,
        grid=(m // tm, nb),
        in_specs=[a_spec, a_spec, w_spec, w_spec,
                  pl.BlockSpec((tm, tn), lambda i, j: (i, j)),
                  pl.BlockSpec((tm, tn), lambda i, j: (i, nb + j))],
        out_specs=pl.BlockSpec((tm, tn), lambda i, j: (i, j)),
        out_shape=jax.ShapeDtypeStruct((m, d), BF16),
        compiler_params=_cparams("parallel", "parallel"),
        name="merge_branches",
    )(yg, ym, w_o_gdn, w_o_mla, gates, gates)


def _small_kernel(h_ref, w_ref, qn_ref, kvn_ref, cos_ref, sin_ref, alog_ref, dtb_ref,
                  cqn_ref, lat_ref, latb_ref, kr_ref, kpad_ref, gb_ref):
    s = jnp.dot(h_ref[...], w_ref[...], preferred_element_type=F32)
    cqn_ref[...] = _rms(s[:, SM_CQ:SM_CKV], qn_ref[...]).astype(cqn_ref.dtype)
    lat = _rms(s[:, SM_CKV:SM_KR], kvn_ref[...])
    lat_ref[...] = lat
    latb_ref[...] = lat.astype(latb_ref.dtype)
    kp = s[:, SM_KR:SM_KRS] * cos_ref[...] + s[:, SM_KRS:SM_AB] * sin_ref[...]
    kr_ref[...] = kp[:, :MLA_ROPE]
    kpad_ref[...] = kp.astype(kpad_ref.dtype)
    ab = s[:, SM_AB:SM_W]
    x = ab + dtb_ref[...]
    softplus = jnp.maximum(x, 0.0) + jnp.log(1.0 + jnp.exp(-jnp.abs(x)))
    g = -jnp.exp(alog_ref[...]) * softplus
    lane = lax.broadcasted_iota(jnp.int32, ab.shape, 1)
    gb_ref[...] = jnp.where(lane < GDN_HEADS, g, _sigmoid(ab))


def small_proj(h, w_small, layer, qn, kvn, cos_t, sin_t, alog, dtb, tm=512):
    m, k = h.shape
    tm = min(tm, m)
    nt = cos_t.shape[0] // tm
    row = lambda n: pl.BlockSpec((tm, n), lambda i: (i, 0))
    vec = lambda n: pl.BlockSpec((1, n), lambda i: (0, 0))
    tab = pl.BlockSpec((tm, LANE), lambda i: (i % nt, 0))
    return pl.pallas_call(
        _small_kernel,
        grid=(m // tm,),
        in_specs=[row(k), pl.BlockSpec((None, k, SM_W), lambda i: (layer, 0, 0)),
                  vec(MLA_Q_LORA), vec(MLA_KV_LORA), tab, tab, vec(LANE), vec(LANE)],
        out_specs=(row(MLA_Q_LORA), row(MLA_KV_LORA), row(MLA_KV_LORA), row(MLA_ROPE), row(LANE), row(LANE)),
        out_shape=(jax.ShapeDtypeStruct((m, MLA_Q_LORA), BF16),
                   jax.ShapeDtypeStruct((m, MLA_KV_LORA), F32),
                   jax.ShapeDtypeStruct((m, MLA_KV_LORA), BF16),
                   jax.ShapeDtypeStruct((m, MLA_ROPE), F32),
                   jax.ShapeDtypeStruct((m, LANE), BF16),
                   jax.ShapeDtypeStruct((m, LANE), F32)),
        compiler_params=_cparams("parallel"),
        name="small_proj",
    )(h, w_small, qn, kvn, cos_t, sin_t, alog, dtb)


def _mlaq_kernel(cqn_ref, wq_ref, wk_ref, cos_ref, sin_ref, ql_ref, qr_ref):
    qa = jnp.dot(cqn_ref[...], wq_ref[...], preferred_element_type=F32)
    cos = cos_ref[...]
    sin = sin_ref[...]
    nope_w = MLA_HEADS * MLA_NOPE
    for h in range(MLA_HEADS):
        lo = h * LANE
        ql = jnp.dot(qa[:, lo:lo + LANE].astype(BF16), wk_ref[h], preferred_element_type=F32)
        ql_ref[h] = (ql * MLA_SCALE).astype(ql_ref.dtype)
        r = qa[:, nope_w + lo:nope_w + lo + LANE] * cos + qa[:, 2 * nope_w + lo:2 * nope_w + lo + LANE] * sin
        qr_ref[h] = (r * MLA_SCALE).astype(qr_ref.dtype)


def mla_query(cqn, wq_all, wk_t, layer, cos_t, sin_t, out_dtype, tm=256):
    m, k = cqn.shape
    tm = min(tm, m)
    nt = cos_t.shape[0] // tm
    nq = wq_all.shape[-1]
    tab = pl.BlockSpec((tm, LANE), lambda i: (i % nt, 0))
    return pl.pallas_call(
        _mlaq_kernel,
        grid=(m // tm,),
        in_specs=[pl.BlockSpec((tm, k), lambda i: (i, 0)),
                  pl.BlockSpec((None, k, nq), lambda i: (layer, 0, 0)),
                  pl.BlockSpec((None, MLA_HEADS, MLA_NOPE, MLA_KV_LORA), lambda i: (layer, 0, 0, 0)),
                  tab, tab],
        out_specs=(pl.BlockSpec((MLA_HEADS, tm, MLA_KV_LORA), lambda i: (0, i, 0)),
                   pl.BlockSpec((MLA_HEADS, tm, LANE), lambda i: (0, i, 0))),
        out_shape=(jax.ShapeDtypeStruct((MLA_HEADS, m, MLA_KV_LORA), out_dtype),
                   jax.ShapeDtypeStruct((MLA_HEADS, m, LANE), out_dtype)),
        compiler_params=_cparams("parallel"),
        name="mla_query",
    )(cqn, wq_all, wk_t, cos_t, sin_t)


def _uv_kernel(o_ref, w_ref, y_ref):
    for h in range(MLA_HEADS):
        y = jnp.dot(o_ref[h].astype(BF16), w_ref[h], preferred_element_type=F32)
        y_ref[:, h * MLA_V:(h + 1) * MLA_V] = y.astype(y_ref.dtype)


def mla_value_up(o_lat, wuv, layer, tm=512):
    _, m, r = o_lat.shape
    tm = min(tm, m)
    return pl.pallas_call(
        _uv_kernel,
        grid=(m // tm,),
        in_specs=[pl.BlockSpec((MLA_HEADS, tm, r), lambda i: (0, i, 0)),
                  pl.BlockSpec((None, MLA_HEADS, r, MLA_V), lambda i: (layer, 0, 0, 0))],
        out_specs=pl.BlockSpec((tm, MLA_HEADS * MLA_V), lambda i: (i, 0)),
        out_shape=jax.ShapeDtypeStruct((m, MLA_HEADS * MLA_V), BF16),
        compiler_params=_cparams("parallel"),
        name="mla_value_up",
    )(o_lat, wuv)


NEG = -1e30


def _attn_prompt_kernel(ql_ref, qr_ref, k_ref, kr_ref, o_ref, m_scr, l_scr, acc_scr, *, tq, tk, nk):
    i = pl.program_id(1)
    kb = pl.program_id(2)
    rows = MLA_HEADS * tq
    last = (i * tq + tq - 1) // tk

    @pl.when(kb == 0)
    def _():
        m_scr[...] = jnp.full_like(m_scr, NEG)
        l_scr[...] = jnp.zeros_like(l_scr)
        acc_scr[...] = jnp.zeros_like(acc_scr)

    def step(masked):
        q = ql_ref[...].reshape(rows, MLA_KV_LORA)
        qr = qr_ref[...].reshape(rows, LANE)
        k = k_ref[...]
        s = (lax.dot_general(q, k, _NT, preferred_element_type=F32)
             + lax.dot_general(qr, kr_ref[...], _NT, preferred_element_type=F32))
        if masked:
            qpos = i * tq + (lax.broadcasted_iota(jnp.int32, s.shape, 0) & (tq - 1))
            kpos = kb * tk + lax.broadcasted_iota(jnp.int32, s.shape, 1)
            s = jnp.where(kpos <= qpos, s, NEG)
        m_old = m_scr[...]
        m_new = jnp.maximum(m_old, jnp.max(s, axis=-1, keepdims=True))
        p = jnp.exp(s - m_new)
        alpha = jnp.exp(m_old - m_new)
        l_scr[...] = alpha * l_scr[...] + jnp.sum(p, axis=-1, keepdims=True)
        acc_scr[...] = alpha * acc_scr[...] + jnp.dot(p.astype(BF16), k, preferred_element_type=F32)
        m_scr[...] = m_new

    @pl.when(kb < last)
    def _():
        step(False)

    @pl.when(kb == last)
    def _():
        step(True)

    @pl.when(kb == nk - 1)
    def _():
        o = acc_scr[...] / l_scr[...]
        o_ref[...] = o.reshape(MLA_HEADS, tq, MLA_KV_LORA).astype(o_ref.dtype)


def attn_prompt(q_lat, q_rope, latb, kpad, bsz, t, tq=128, tk=1024):
    nq, nk = t // tq, t // tk
    rows = MLA_HEADS * tq
    q4 = q_lat.reshape(MLA_HEADS, bsz, t, MLA_KV_LORA)
    r4 = q_rope.reshape(MLA_HEADS, bsz, t, LANE)
    k3 = latb.reshape(bsz, t, MLA_KV_LORA)
    kr3 = kpad.reshape(bsz, t, LANE)
    kmap = lambda b, i, kb: (b, jnp.minimum(kb, (i * tq + tq - 1) // tk), 0)
    out = pl.pallas_call(
        functools.partial(_attn_prompt_kernel, tq=tq, tk=tk, nk=nk),
        grid=(bsz, nq, nk),
        in_specs=[pl.BlockSpec((MLA_HEADS, None, tq, MLA_KV_LORA), lambda b, i, kb: (0, b, i, 0)),
                  pl.BlockSpec((MLA_HEADS, None, tq, LANE), lambda b, i, kb: (0, b, i, 0)),
                  pl.BlockSpec((None, tk, MLA_KV_LORA), kmap),
                  pl.BlockSpec((None, tk, LANE), kmap)],
        out_specs=pl.BlockSpec((MLA_HEADS, None, tq, MLA_KV_LORA), lambda b, i, kb: (0, b, i, 0)),
        out_shape=jax.ShapeDtypeStruct((MLA_HEADS, bsz, t, MLA_KV_LORA), BF16),
        scratch_shapes=[pltpu.VMEM((rows, 1), F32), pltpu.VMEM((rows, 1), F32),
                        pltpu.VMEM((rows, MLA_KV_LORA), F32)],
        compiler_params=_cparams("parallel", "parallel", "arbitrary"),
        name="attn_prompt",
    )(q4, r4, k3, kr3)
    return out.reshape(MLA_HEADS, bsz * t, MLA_KV_LORA)


def _attn_sample_kernel(pt_ref, ql_ref, qr_ref, *rest, pg, t, ng, nreq):
    np_ = nreq * pg
    lat_refs = rest[:np_]
    krt_refs = rest[np_:2 * np_]
    ln_ref, kn_ref, o_ref, m_scr, l_scr, acc_scr = rest[2 * np_:]
    j = pl.program_id(1)
    rows = MLA_HEADS * t
    reqs = range(nreq)

    @pl.when(j == 0)
    def _():
        m_scr[...] = jnp.full_like(m_scr, NEG)
        l_scr[...] = jnp.zeros_like(l_scr)
        acc_scr[...] = jnp.zeros_like(acc_scr)

    q = [ql_ref[:, a * t:(a + 1) * t, :].reshape(rows, MLA_KV_LORA).astype(BF16) for a in reqs]
    qr = [qr_ref[:, a * t:(a + 1) * t, :].reshape(rows, LANE)[:, :MLA_ROPE].astype(BF16) for a in reqs]

    def update(s, v):
        m_old = [m_scr[a] for a in reqs]
        m_new = [jnp.maximum(m_old[a], jnp.max(s[a], axis=-1, keepdims=True)) for a in reqs]
        p = [jnp.exp(s[a] - m_new[a]) for a in reqs]
        alpha = [jnp.exp(m_old[a] - m_new[a]) for a in reqs]
        pv = [jnp.dot(p[a].astype(BF16), v[a], preferred_element_type=F32) for a in reqs]
        for a in reqs:
            l_scr[a] = alpha[a] * l_scr[a] + jnp.sum(p[a], axis=-1, keepdims=True)
            acc_scr[a] = alpha[a] * acc_scr[a] + pv[a]
            m_scr[a] = m_new[a]

    k = [jnp.concatenate([r[...].astype(BF16) for r in lat_refs[a * pg:(a + 1) * pg]], axis=0) for a in reqs]
    krt = [jnp.concatenate([r[...].astype(BF16) for r in krt_refs[a * pg:(a + 1) * pg]], axis=1) for a in reqs]
    s = [lax.dot_general(q[a], k[a], _NT, preferred_element_type=F32)
         + jnp.dot(qr[a], krt[a], preferred_element_type=F32) for a in reqs]
    update(s, k)

    @pl.when(j == ng - 1)
    def _():
        pad = 2 * SUBLANE - t
        kn = [jnp.concatenate([ln_ref[a * t:(a + 1) * t, :], jnp.zeros((pad, MLA_KV_LORA), F32)],
                              axis=0).astype(BF16) for a in reqs]
        krn = [jnp.concatenate([kn_ref[a * t:(a + 1) * t, :], jnp.zeros((pad, MLA_ROPE), F32)],
                               axis=0).astype(BF16) for a in reqs]
        sn = [lax.dot_general(q[a], kn[a], _NT, preferred_element_type=F32)
              + lax.dot_general(qr[a], krn[a], _NT, preferred_element_type=F32) for a in reqs]
        qt = lax.broadcasted_iota(jnp.int32, sn[0].shape, 0) & (t - 1)
        kt = lax.broadcasted_iota(jnp.int32, sn[0].shape, 1)
        update([jnp.where(kt <= qt, sn[a], NEG) for a in reqs], kn)
        for a in reqs:
            o = acc_scr[a] / l_scr[a]
            o_ref[:, a * t:(a + 1) * t, :] = o.reshape(MLA_HEADS, t, MLA_KV_LORA).astype(o_ref.dtype)


def attn_sample(q_lat, q_rope, cache_lat, cache_krt, page_table, lat_new, kr_new, layer, db, t, pg=16, nreq=4):
    n_pages = page_table.shape[1]
    ng = n_pages // pg
    pt = page_table.reshape(-1)
    rows = MLA_HEADS * t

    def page_spec(shape, a, p):
        return pl.BlockSpec((None, None) + shape,
                            lambda b, j, pt_ref: (layer, pt_ref[(b * nreq + a) * n_pages + j * pg + p], 0, 0))

    pages = [(a, p) for a in range(nreq) for p in range(pg)]
    grid_spec = pltpu.PrefetchScalarGridSpec(
        num_scalar_prefetch=1,
        grid=(db // nreq, ng),
        in_specs=([pl.BlockSpec((MLA_HEADS, nreq * t, MLA_KV_LORA), lambda b, j, pt_ref: (0, b, 0)),
                   pl.BlockSpec((MLA_HEADS, nreq * t, LANE), lambda b, j, pt_ref: (0, b, 0))]
                  + [page_spec((PAGE_SIZE, MLA_KV_LORA), a, p) for a, p in pages]
                  + [page_spec((MLA_ROPE, PAGE_SIZE), a, p) for a, p in pages]
                  + [pl.BlockSpec((nreq * t, MLA_KV_LORA), lambda b, j, pt_ref: (b, 0)),
                     pl.BlockSpec((nreq * t, MLA_ROPE), lambda b, j, pt_ref: (b, 0))]),
        out_specs=pl.BlockSpec((MLA_HEADS, nreq * t, MLA_KV_LORA), lambda b, j, pt_ref: (0, b, 0)),
        scratch_shapes=[pltpu.VMEM((nreq, rows, 1), F32), pltpu.VMEM((nreq, rows, 1), F32),
                        pltpu.VMEM((nreq, rows, MLA_KV_LORA), F32)],
    )
    n = nreq * pg
    return pl.pallas_call(
        functools.partial(_attn_sample_kernel, pg=pg, t=t, ng=ng, nreq=nreq),
        grid_spec=grid_spec,
        out_shape=jax.ShapeDtypeStruct((MLA_HEADS, db * t, MLA_KV_LORA), F32),
        compiler_params=_cparams("parallel", "arbitrary"),
        name="attn_sample",
    )(pt, q_lat, q_rope, *([cache_lat] * n), *([cache_krt] * n), lat_new, kr_new)


def _cumsum_rows(x, seg):
    row = lax.broadcasted_iota(jnp.int32, x.shape, 0) & (seg - 1)
    s = 1
    while s < seg:
        x = x + jnp.where(row >= s, pltpu.roll(x, s, 0), 0.0)
        s *= 2
    return x


def _col(x, lane):
    li = lax.broadcasted_iota(jnp.int32, x.shape, 1)
    return jnp.sum(jnp.where(li == lane, x, 0.0), axis=1, keepdims=True)


def _rowvec(xt, sub):
    si = lax.broadcasted_iota(jnp.int32, xt.shape, 0)
    return jnp.sum(jnp.where(si == sub, xt, 0.0), axis=0, keepdims=True)


def _transpose_block(x):
    c = x.shape[0]
    if c < LANE:
        x = jnp.concatenate([x, jnp.zeros((LANE - c, LANE), x.dtype)], axis=0)
    return x.T[:, :c]


def _l2norm_heads(x, nh):
    outs = []
    for j in range(nh):
        xh = x[:, j * LANE:(j + 1) * LANE]
        outs.append(xh * lax.rsqrt(jnp.sum(xh * xh, axis=-1, keepdims=True) + EPS))
    return outs


def _chunk_masks(c, same, nlevel):
    ri = lax.broadcasted_iota(jnp.int32, (c, c), 0)
    ci = lax.broadcasted_iota(jnp.int32, (c, c), 1)
    tri = ri >= ci
    strict = ri > ci
    if same is not None:
        tri = jnp.logical_and(tri, same)
        strict = jnp.logical_and(strict, same)
    lower = [jnp.logical_and((ri >> k) == (ci >> k) + 1, (ri >> (k + 1)) == (ci >> (k + 1)))
             for k in range(nlevel)]
    return tri, strict, ri == ci, lower


def _chunk_prep(q, k, v, gcol, grow, bcol, glast, masks):
    tri, strict, diag, lower = masks
    n = range(len(q))
    qs = [q[i] * (GDN_DK ** -0.5) for i in n]
    kb = [k[i].astype(BF16) for i in n]
    kk = [lax.dot_general(kb[i], kb[i], _NT, preferred_element_type=F32) for i in n]
    qk = [lax.dot_general(qs[i].astype(BF16), kb[i], _NT, preferred_element_type=F32) for i in n]
    decay = [jnp.where(tri, jnp.exp(jnp.where(tri, gcol[i] - grow[i], 0.0)), 0.0) for i in n]
    lmat = [jnp.where(strict, bcol[i] * kk[i] * decay[i], 0.0) for i in n]
    eye = jnp.where(diag, 1.0, 0.0)
    x = [eye - jnp.where(lower[0], lmat[i], 0.0) for i in n]
    for m in lower[1:]:
        xb = [x[i].astype(BF16) for i in n]
        t = [jnp.dot(xb[i], jnp.where(m, lmat[i], 0.0).astype(BF16), preferred_element_type=F32) for i in n]
        x = [x[i] - jnp.dot(t[i].astype(BF16), xb[i], preferred_element_type=F32) for i in n]
    eg = [jnp.exp(gcol[i]) for i in n]
    uw = [_bdot(x[i], jnp.concatenate([v[i] * bcol[i], k[i] * (bcol[i] * eg[i])], axis=1)) for i in n]
    u = [uw[i][:, :GDN_DV] for i in n]
    w = [uw[i][:, GDN_DV:] for i in n]
    qkd = [qk[i] * decay[i] for i in n]
    qd = [qs[i] * eg[i] for i in n]
    kd = [k[i] * jnp.exp(glast[i] - gcol[i]) for i in n]
    return u, w, qkd, qd, kd


def _gated_out(o, z, nw):
    y = o * lax.rsqrt(jnp.mean(o * o, axis=-1, keepdims=True) + EPS) * nw
    return y * (z * _sigmoid(z))


def _gdn_prompt_kernel(q_ref, k_ref, v_ref, z_ref, hq_ref, hk_ref, hv_ref, cwq_ref, cwk_ref, cwv_ref,
                       gb_ref, nw_ref, y_ref, so_ref, s_scr, *, hb, rows, nr):
    hg = pl.program_id(1)
    r = pl.program_id(2)
    c = GDN_CHUNK

    @pl.when(r == 0)
    def _():
        s_scr[...] = jnp.zeros_like(s_scr)

    def conv(x_ref, h_ref, w_ref):
        u = x_ref[...]
        halo = jnp.where(r > 0, h_ref[...], 0.0)
        w = w_ref[...]
        acc = u * w[GDN_CONV - 1:GDN_CONV, :]
        row8 = lax.broadcasted_iota(jnp.int32, halo.shape, 0)
        for s in range(1, GDN_CONV):
            ru = pltpu.roll(u, s, 0)
            top = jnp.where(row8 < s, pltpu.roll(halo, s, 0), ru[:SUBLANE])
            sh = jnp.concatenate([top, ru[SUBLANE:]], axis=0)
            acc = acc + sh * w[GDN_CONV - 1 - s:GDN_CONV - s, :]
        return acc * _sigmoid(acc)

    qh = _l2norm_heads(conv(q_ref, hq_ref, cwq_ref), hb)
    kh = _l2norm_heads(conv(k_ref, hk_ref, cwk_ref), hb)
    vv = conv(v_ref, hv_ref, cwv_ref)
    gb = gb_ref[...]
    gc_all = _cumsum_rows(gb, c)
    nw = nw_ref[...]
    masks = _chunk_masks(c, None, c.bit_length() - 1)
    nc = rows // c
    qs_, ks_, vs_, gcol, grow, bcol, glast = [], [], [], [], [], [], []
    for ch in range(nc):
        sl = slice(ch * c, (ch + 1) * c)
        gc = gc_all[sl]
        gct = _transpose_block(gc)
        for j in range(hb):
            head = hg * hb + j
            qs_.append(qh[j][sl])
            ks_.append(kh[j][sl])
            vs_.append(vv[sl, j * LANE:(j + 1) * LANE])
            gcol.append(_col(gc, head))
            bcol.append(_col(gb[sl], GDN_HEADS + head))
            grow.append(_rowvec(gct, head))
            glast.append(gcol[-1][c - 1:c, :])
    u, w, qkd, qd, kd = _chunk_prep(qs_, ks_, vs_, gcol, grow, bcol, glast, masks)
    heads = range(hb)
    state = [s_scr[j] for j in heads]
    for ch in range(nc):
        sl = slice(ch * c, (ch + 1) * c)
        g = [ch * hb + j for j in heads]
        wq = [_bdot(jnp.concatenate([w[g[j]], qd[g[j]]], axis=0), state[j]) for j in heads]
        v_new = [u[g[j]] - wq[j][:c] for j in heads]
        o = [wq[j][c:] + _bdot(qkd[g[j]], v_new[j]) for j in heads]
        state = [state[j] * jnp.exp(glast[g[j]]) + lax.dot_general(
            kd[g[j]].astype(BF16), v_new[j].astype(BF16), _TN, preferred_element_type=F32) for j in heads]
        for j in heads:
            y_ref[sl, j * LANE:(j + 1) * LANE] = _gated_out(
                o[j], z_ref[sl, j * LANE:(j + 1) * LANE], nw).astype(y_ref.dtype)
    for j in heads:
        s_scr[j] = state[j]

    @pl.when(r == nr - 1)
    def _():
        so_ref[...] = s_scr[...]


def gdn_prompt(qkvz, gb, conv_w, norm_w, layer, bsz, t, hb=4, rows=512):
    w = hb * LANE
    nr = t // rows
    x3 = qkvz.reshape(bsz, t, -1)
    gb3 = gb.reshape(bsz, t, LANE)
    kq, kv_, kz = GDN_QK_W // w, 2 * GDN_QK_W // w, CONV_DIM // w
    hr = rows // SUBLANE

    def xs(off):
        return pl.BlockSpec((None, rows, w), lambda b, g, r: (b, r, off + g))

    def hs(off):
        return pl.BlockSpec((None, SUBLANE, w), lambda b, g, r: (b, jnp.maximum(r * hr - 1, 0), off + g))

    def ws(off):
        return pl.BlockSpec((None, GDN_CONV, w), lambda b, g, r: (layer, 0, off + g))

    return pl.pallas_call(
        functools.partial(_gdn_prompt_kernel, hb=hb, rows=rows, nr=nr),
        grid=(bsz, GDN_HEADS // hb, nr),
        in_specs=[xs(0), xs(kq), xs(kv_), xs(kz), hs(0), hs(kq), hs(kv_), ws(0), ws(kq), ws(kv_),
                  pl.BlockSpec((None, rows, LANE), lambda b, g, r: (b, r, 0)),
                  pl.BlockSpec((1, GDN_DV), lambda b, g, r: (0, 0))],
        out_specs=(pl.BlockSpec((None, rows, w), lambda b, g, r: (b, r, g)),
                   pl.BlockSpec((None, hb, GDN_DK, GDN_DV), lambda b, g, r: (b, g, 0, 0))),
        out_shape=(jax.ShapeDtypeStruct((bsz, t, GDN_V_W), BF16),
                   jax.ShapeDtypeStruct((bsz, GDN_HEADS, GDN_DK, GDN_DV), F32)),
        scratch_shapes=[pltpu.VMEM((hb, GDN_DK, GDN_DV), F32)],
        compiler_params=_cparams("parallel", "parallel", "arbitrary"),
        name="gdn_prompt",
    )(x3, x3, x3, x3, x3, x3, x3, conv_w, conv_w, conv_w, gb3, norm_w)


def _gdn_sample_kernel(q_ref, k_ref, v_ref, z_ref, hq_ref, hk_ref, hv_ref, cwq_ref, cwk_ref, cwv_ref,
                       gb_ref, nw_ref, s0_ref, y_ref, so_ref, *, hb, t, nb):
    hg = pl.program_id(1)
    c = nb * t

    def conv(x_ref, h_ref, w_ref):
        u = x_ref[...]
        halo = h_ref[...]
        w = w_ref[...]
        acc = u * w[GDN_CONV - 1:GDN_CONV, :]
        tt = lax.broadcasted_iota(jnp.int32, u.shape, 0) & (t - 1)
        for s in range(1, GDN_CONV):
            sh = jnp.where(tt < s, pltpu.roll(halo, c - t + s, 0), pltpu.roll(u, s, 0))
            acc = acc + sh * w[GDN_CONV - 1 - s:GDN_CONV - s, :]
        return acc * _sigmoid(acc)

    qh = _l2norm_heads(conv(q_ref, hq_ref, cwq_ref), hb)
    kh = _l2norm_heads(conv(k_ref, hk_ref, cwk_ref), hb)
    vv = conv(v_ref, hv_ref, cwv_ref)
    gb = gb_ref[...]
    gc = _cumsum_rows(gb, t)
    gl = jnp.broadcast_to(gc.reshape(nb, t, LANE)[:, t - 1:t, :], (nb, t, LANE)).reshape(c, LANE)
    gct = _transpose_block(gc)
    nw = nw_ref[...]
    shift = t.bit_length() - 1
    same = (lax.broadcasted_iota(jnp.int32, (c, c), 0) >> shift) == (lax.broadcasted_iota(jnp.int32, (c, c), 1) >> shift)
    masks = _chunk_masks(c, same, shift)
    rowreq = lax.broadcasted_iota(jnp.int32, (c, LANE), 0) >> shift
    heads = range(hb)
    reqs = range(nb)
    gcol = [_col(gc, hg * hb + j) for j in heads]
    bcol = [_col(gb, GDN_HEADS + hg * hb + j) for j in heads]
    grow = [_rowvec(gct, hg * hb + j) for j in heads]
    glast = [_col(gl, hg * hb + j) for j in heads]
    u, w, qkd, qd, kd = _chunk_prep(qh, kh, [vv[:, j * LANE:(j + 1) * LANE] for j in heads],
                                    gcol, grow, bcol, glast, masks)
    wq = [[_bdot(jnp.concatenate([w[j][b * t:(b + 1) * t], qd[j][b * t:(b + 1) * t]], axis=0), s0_ref[b, j])
           for b in reqs] for j in heads]
    v_new = [jnp.concatenate([u[j][b * t:(b + 1) * t] - wq[j][b][:t] for b in reqs], axis=0) for j in heads]
    o = [jnp.concatenate([wq[j][b][t:] for b in reqs], axis=0) + _bdot(qkd[j], v_new[j]) for j in heads]
    for j in heads:
        vb = v_new[j].astype(BF16)
        for b in reqs:
            kdb = jnp.where(rowreq == b, kd[j], 0.0).astype(BF16)
            so_ref[b, j] = s0_ref[b, j] * jnp.exp(glast[j][b * t:b * t + 1, :]) + lax.dot_general(
                kdb, vb, _TN, preferred_element_type=F32)
        y_ref[:, j * LANE:(j + 1) * LANE] = _gated_out(
            o[j], z_ref[:, j * LANE:(j + 1) * LANE], nw).astype(y_ref.dtype)


def gdn_sample(qkvz, gb, conv_w, norm_w, halo, state, layer, db, t, hb=8, nb=8):
    w = hb * LANE
    c = nb * t
    kq, kv_, kz = GDN_QK_W // w, 2 * GDN_QK_W // w, CONV_DIM // w

    def xs(off):
        return pl.BlockSpec((c, w), lambda i, g: (i, off + g))

    def ws(off):
        return pl.BlockSpec((None, GDN_CONV, w), lambda i, g: (layer, 0, off + g))

    return pl.pallas_call(
        functools.partial(_gdn_sample_kernel, hb=hb, t=t, nb=nb),
        grid=(db // nb, GDN_HEADS // hb),
        in_specs=[xs(0), xs(kq), xs(kv_), xs(kz), xs(0), xs(kq), xs(kv_), ws(0), ws(kq), ws(kv_),
                  pl.BlockSpec((c, LANE), lambda i, g: (i, 0)),
                  pl.BlockSpec((1, GDN_DV), lambda i, g: (0, 0)),
                  pl.BlockSpec((None, nb, hb, GDN_DK, GDN_DV), lambda i, g: (layer, i, g, 0, 0))],
        out_specs=(pl.BlockSpec((c, w), lambda i, g: (i, g)),
                   pl.BlockSpec((nb, hb, GDN_DK, GDN_DV), lambda i, g: (i, g, 0, 0))),
        out_shape=(jax.ShapeDtypeStruct((db * t, GDN_V_W), BF16),
                   jax.ShapeDtypeStruct((db, GDN_HEADS, GDN_DK, GDN_DV), F32)),
        compiler_params=_cparams("parallel", "parallel"),
        name="gdn_sample",
    )(qkvz, qkvz, qkvz, qkvz, halo, halo, halo, conv_w, conv_w, conv_w, gb, norm_w, state)


def _swap_halves(x):
    half = x.shape[-1] // 2
    return jnp.concatenate([x[..., half:], x[..., :half]], axis=-1)


def _pack_small(w_in):
    depth, d, _ = w_in.shape
    o_a = CONV_DIM + GDN_V_W
    o_b = o_a + GDN_HEADS
    o_cq = o_b + GDN_HEADS
    o_ckv = o_cq + MLA_Q_LORA
    o_kr = o_ckv + MLA_KV_LORA
    o_g = o_kr + MLA_ROPE
    z = lambda n: jnp.zeros((depth, d, n), w_in.dtype)
    kr = w_in[:, :, o_kr:o_g]
    small = jnp.concatenate([
        w_in[:, :, o_cq:o_ckv], w_in[:, :, o_ckv:o_kr],
        kr, z(LANE - MLA_ROPE), _swap_halves(kr), z(LANE - MLA_ROPE),
        w_in[:, :, o_a:o_b], w_in[:, :, o_b:o_cq], z(LANE - 2 * GDN_HEADS)], axis=-1).astype(BF16)
    gates = w_in[:, :, o_g:].astype(BF16)
    qkvz = w_in[:, :, :o_a].astype(BF16)
    return small, gates, qkvz


def _pack_mla(w_q_b, w_kv_b):
    depth = w_q_b.shape[0]
    wq = w_q_b.reshape(depth, MLA_Q_LORA, MLA_HEADS, MLA_NOPE + MLA_ROPE)
    nope = wq[..., :MLA_NOPE].reshape(depth, MLA_Q_LORA, -1)
    rope = wq[..., MLA_NOPE:]
    pad = jnp.zeros(rope.shape[:-1] + (LANE - MLA_ROPE,), rope.dtype)
    rope_p = jnp.concatenate([rope, pad], axis=-1).reshape(depth, MLA_Q_LORA, -1)
    rope_s = jnp.concatenate([_swap_halves(rope), pad], axis=-1).reshape(depth, MLA_Q_LORA, -1)
    wq_all = jnp.concatenate([nope, rope_p, rope_s], axis=-1).astype(BF16)
    wkv = w_kv_b.reshape(depth, MLA_KV_LORA, MLA_HEADS, MLA_NOPE + MLA_V)
    wk_t = wkv[..., :MLA_NOPE].transpose(0, 2, 3, 1).astype(BF16)
    wuv = wkv[..., MLA_NOPE:].transpose(0, 2, 1, 3).astype(BF16)
    return wq_all, wk_t, wuv


def _rope_tables(pos):
    half = MLA_ROPE // 2
    inv_freq = ROPE_THETA ** (-jnp.arange(half, dtype=F32) / half)
    ang = pos.astype(F32)[:, None] * inv_freq[None, :]
    cos, sin = jnp.cos(ang), jnp.sin(ang)
    z = jnp.zeros((pos.shape[0], LANE - MLA_ROPE), F32)
    return jnp.concatenate([cos, cos, z], axis=-1), jnp.concatenate([-sin, sin, z], axis=-1)


def _lane_pad(v, n=LANE):
    return jnp.concatenate([v, jnp.zeros((n - v.shape[0],), v.dtype)]).reshape(1, n)


def _layer(x, h, mods, l, wts, tabs, group):
    bsz, t, d = x.shape
    m = bsz * t
    sh_a, sc_a, gt_a, sh_m, sc_m, gt_m = mods[l]
    h2 = h.reshape(m, d)
    qkvz = matmul(h2, wts["w_qkvz"], l, CONV_DIM + GDN_V_W, tn=1024, name="in_proj_gdn")
    gates = matmul(h2, wts["w_gates"], l, 2 * d, tn=1024, out_dtype=BF16, act="sigmoid", name="in_proj_gates")
    cos_t, sin_t = tabs
    cqn, lat, latb, kr, kpad, gb = small_proj(
        h2, wts["w_small"], l, wts["q_a_norm"][l:l + 1], wts["kv_a_norm"][l:l + 1], cos_t, sin_t,
        wts["alog"][l], wts["dtb"][l])
    norm_w = wts["gdn_norm"][l:l + 1]
    if group["kind"] == "prompt":
        y_gdn, s_new = gdn_prompt(qkvz, gb, wts["conv_w"], norm_w, l, bsz, t)
        y_gdn = y_gdn.reshape(m, GDN_V_W)
        q_lat, q_rope = mla_query(cqn, wts["wq_all"], wts["wk_t"], l, cos_t, sin_t, BF16)
        o_lat = attn_prompt(q_lat, q_rope, latb, kpad, bsz, t)
    else:
        y_gdn, s_new = gdn_sample(qkvz, gb, wts["conv_w"], norm_w, group["halo"][l], group["state"], l, bsz, t)
        q_lat, q_rope = mla_query(cqn, wts["wq_all"], wts["wk_t"], l, cos_t, sin_t, F32)
        o_lat = attn_sample(q_lat, q_rope, group["cache_lat"], group["cache_kr"], group["page_table"],
                            lat, kr, l, bsz, t)
    y_mla = mla_value_up(o_lat, wts["wuv"], l)
    merged = merge_branches(y_gdn, y_mla, wts["w_o_gdn"], wts["w_o_mla"], gates, l)
    mix = matmul(merged, wts["w_out"], l, d, name="out_proj").reshape(bsz, t, d)
    g4 = wts["norm_g"][l]
    x, hm = resid_norm(x, mix, g4, 1, gt_a, pre=(g4, 2, sc_m, sh_m))
    f = matmul(hm.reshape(m, d), wts["w_up"], l, wts["w_up"].shape[-1], out_dtype=BF16, act="relu2", name="ffn_up")
    f = matmul_acc(f, wts["w_down"], l, name="ffn_down").reshape(bsz, t, d)
    if l + 1 < len(mods):
        nsh, nsc = mods[l + 1][0], mods[l + 1][1]
        x, h_next = resid_norm(x, f, g4, 3, gt_m, pre=(wts["norm_g"][l + 1], 0, nsc, nsh))
    else:
        x, h_next = resid_norm(x, f, g4, 3, gt_m), None
    conv_new = qkvz.reshape(bsz, t, -1)[:, t - (GDN_CONV - 1):, :CONV_DIM]
    return x, h_next, (lat.reshape(bsz, t, -1), kr.reshape(bsz, t, -1), s_new, conv_new)


def kernel(x_prompt, x_sample, cache_mla_latent, cache_mla_krope, state_gdn, state_conv, page_table, c_prompt, c_sample, w_in, conv_w, a_log, dt_bias, gdn_norm, q_a_norm, w_q_b, kv_a_norm, w_kv_b, w_o_gdn, w_o_mla, w_out, w_up, w_down, w_ada, b_ada, norm_g):
    bsz, seq, d = x_prompt.shape
    db, t = x_sample.shape[0], x_sample.shape[1]
    depth = w_in.shape[0]
    past_len = page_table.shape[1] * PAGE_SIZE

    w_small, w_gates, w_qkvz = _pack_small(w_in)
    wq_all, wk_t, wuv = _pack_mla(w_q_b, w_kv_b)
    wts = dict(w_qkvz=w_qkvz, w_gates=w_gates, w_small=w_small, q_a_norm=q_a_norm, kv_a_norm=kv_a_norm,
               alog=jnp.stack([_lane_pad(a_log[l]) for l in range(depth)]),
               dtb=jnp.stack([_lane_pad(dt_bias[l]) for l in range(depth)]),
               gdn_norm=gdn_norm, conv_w=conv_w, wq_all=wq_all, wk_t=wk_t, wuv=wuv,
               w_o_gdn=w_o_gdn, w_o_mla=w_o_mla, w_out=w_out, w_up=w_up, w_down=w_down, norm_g=norm_g)

    tabs_p = _rope_tables(jnp.arange(seq, dtype=jnp.int32))
    rows_s = min(512, db * t)
    tabs_s = _rope_tables(past_len + (jnp.arange(rows_s, dtype=jnp.int32) % t))

    nc = bsz + db
    mc = -(-nc // 16) * 16
    c_all = jnp.concatenate([c_prompt, c_sample, jnp.zeros((mc - nc, d), F32)], axis=0)
    mod = ada_mod(c_all, w_ada, b_ada)

    def split_mods(lo, n):
        return [tuple(mod[l, lo:lo + n, i * d:(i + 1) * d].reshape(n, 1, d) for i in range(N_MOD))
                for l in range(depth)]

    mods_p = split_mods(0, bsz)
    mods_s = split_mods(bsz, db)

    halo = jnp.pad(state_conv, ((0, 0), (0, 0), (t - (GDN_CONV - 1), 0), (0, 0))).reshape(depth, db * t, CONV_DIM)
    grp_p = dict(kind="prompt")
    grp_s = dict(kind="sample", halo=halo, state=state_gdn, cache_lat=cache_mla_latent,
                 cache_kr=jnp.swapaxes(cache_mla_krope, 2, 3), page_table=page_table)

    xp, xs = x_prompt, x_sample
    hp = norm_mod(xp, norm_g[0], 0, mods_p[0][1], mods_p[0][0])
    hs = norm_mod(xs, norm_g[0], 0, mods_s[0][1], mods_s[0][0])
    leaves_p, leaves_s = [], []
    for l in range(depth):
        xp, hp, lp = _layer(xp, hp, mods_p, l, wts, tabs_p, grp_p)
        xs, hs, ls = _layer(xs, hs, mods_s, l, wts, tabs_s, grp_s)
        leaves_p.append(lp)
        leaves_s.append(ls)
    stack = lambda leaves, i: jnp.stack([lv[i] for lv in leaves])
    return (xp, xs, stack(leaves_p, 0), stack(leaves_p, 1), stack(leaves_p, 2), stack(leaves_p, 3),
            stack(leaves_s, 0), stack(leaves_s, 1), stack(leaves_s, 2), stack(leaves_s, 3))
```
